```python
import jax, jax.numpy as jnp
from jax import lax
import numpy as np

D_MODEL = 4096
BATCH = 4
SEQ = 4096
DEPTH = 4
DEC_BATCH = 8
DEC_SEQ = 2048
PAST_LEN = 128

HEAD_DIM = 128
ROPE_THETA = 10000.0
NORM_EPS = 1e-6
NEG_INF = -1e30
GRID_W = 64

NA_HEADS = 8
NA_ROWS = 8
NA_COLS = 16
NA_QCOLS = NA_COLS
NA_KCOLS = 2 * NA_COLS
NA_NCB = GRID_W // NA_QCOLS
SW_HEADS = 8
SW_KV_HEADS = 2
SW_GROUP = SW_HEADS // SW_KV_HEADS
SW_WINDOW = 128
DIL_PAIRS = ((128, 1), (512, 4), (2048, 16))
DIL_HEADS_PER_GROUP = 3
DIL_HEADS = DIL_HEADS_PER_GROUP * len(DIL_PAIRS)
MLA_HEADS = 7
MLA_Q_RANK = 896
MLA_KV_RANK = 320
MLA_NOPE = 128
MLA_ROPE = 64
MLA_V = 128
MLA_Q_BLK = 128

A_W = NA_HEADS * HEAD_DIM
B_QW = SW_HEADS * HEAD_DIM
B_KW = SW_KV_HEADS * HEAD_DIM
C_W = DIL_HEADS * HEAD_DIM
D_OUT_W = MLA_HEADS * MLA_V
SPLITS = (A_W, A_W, A_W, B_QW, B_KW, B_KW, C_W, C_W, C_W, MLA_Q_RANK, MLA_KV_RANK, MLA_ROPE)
SPLIT_POINTS = [int(i) for i in np.cumsum(SPLITS)[:-1]]
IN_W = int(sum(SPLITS))
MIX_W = A_W + B_QW + C_W + D_OUT_W
D_FF = 11008

kernel_name = "hybrid_parallel_head_group_encoder"


def rms_norm(x, g):
    xf = x.astype(jnp.float32)
    y = xf * lax.rsqrt(jnp.mean(xf * xf, axis=-1, keepdims=True) + NORM_EPS)
    return (y * g.astype(jnp.float32)).astype(x.dtype)


def rotary(x):
    s_len, d = x.shape[1], x.shape[-1]
    inv = 1.0 / (ROPE_THETA ** (jnp.arange(0, d, 2, dtype=jnp.float32) / d))
    ang = jnp.arange(s_len, dtype=jnp.float32)[:, None] * inv[None, :]
    ang = ang.reshape((1, s_len) + (1,) * (x.ndim - 3) + (d // 2,))
    cos, sin = jnp.cos(ang), jnp.sin(ang)
    xf = x.astype(jnp.float32)
    x1, x2 = xf[..., : d // 2], xf[..., d // 2:]
    return jnp.concatenate([x1 * cos - x2 * sin, x2 * cos + x1 * sin], axis=-1).astype(x.dtype)


def banded_attention(q, k, v, half_w, sink=None):
    n, L, hk, g, hd = q.shape
    blk = half_w
    nb = -(-L // blk)
    lp = nb * blk
    q = jnp.pad(q, ((0, 0), (0, lp - L), (0, 0), (0, 0), (0, 0)))
    pad_kv = ((0, 0), (blk, lp - L + blk), (0, 0), (0, 0))

    def windows(t):
        tb = jnp.pad(t, pad_kv).reshape(n, nb + 2, blk, hk, hd)
        return jnp.concatenate([tb[:, :-2], tb[:, 1:-1], tb[:, 2:]], axis=2)

    kw, vw = windows(k), windows(v)
    qb = q.reshape(n, nb, blk, hk, g, hd)
    s = jnp.einsum('nbqhgd,nbkhd->nbhgqk', qb, kw).astype(jnp.float32) * (hd ** -0.5)
    qpos = np.arange(lp).reshape(nb, blk)
    kpos = (np.arange(nb)[:, None] - 1) * blk + np.arange(3 * blk)[None, :]
    rel = kpos[:, None, :] - qpos[:, :, None]
    valid = (np.abs(rel) <= half_w) & (kpos[:, None, :] >= 0) & (kpos[:, None, :] < L)
    s = jnp.where(valid[None, :, None, None], s, NEG_INF)
    m = jnp.max(s, axis=-1)
    if sink is not None:
        sk = sink.astype(jnp.float32)[None, None, :, :, None]
        m = jnp.maximum(m, sk)
    p = jnp.exp(s - m[..., None])
    l = jnp.sum(p, axis=-1)
    if sink is not None:
        l = l + jnp.exp(sk - m)
    o = jnp.einsum('nbhgqk,nbkhd->nbqhgd', p.astype(v.dtype), vw).astype(jnp.float32)
    l_t = l.transpose(0, 1, 4, 2, 3)
    o = (o / l_t[..., None]).astype(v.dtype).reshape(n, lp, hk, g, hd)[:, :L]
    lse = (m.transpose(0, 1, 4, 2, 3) + jnp.log(l_t)).reshape(n, lp, hk, g)[:, :L]
    return o, lse


def dilated_attention(q, k, v, half_w, dil):
    b, s_len, h, hd = q.shape
    n_side = half_w // dil
    sub = s_len // dil

    def to_sub(t):
        return t.reshape(b, sub, dil, h, hd).transpose(0, 2, 1, 3, 4).reshape(b * dil, sub, h, hd)

    o, lse = banded_attention(to_sub(q)[:, :, :, None], to_sub(k), to_sub(v), n_side)
    o = o[:, :, :, 0].reshape(b, dil, sub, h, hd).transpose(0, 2, 1, 3, 4).reshape(b, s_len, h, hd)
    lse = lse[..., 0].reshape(b, dil, sub, h).transpose(0, 2, 1, 3).reshape(b, s_len, h)
    return o, lse


def neighborhood_attention(q, k, v, rpb):
    b, s_len, h, hd = q.shape
    rows = s_len // GRID_W
    kr = min(NA_ROWS, rows)
    row_start = np.clip(np.arange(rows) - kr // 2, 0, rows - kr).astype(np.int32)
    col_start = np.clip(np.arange(NA_NCB) * NA_QCOLS - NA_COLS // 2, 0, GRID_W - NA_KCOLS)
    kc_idx = col_start[:, None] + np.arange(NA_KCOLS)[None, :]
    qc = np.arange(GRID_W).reshape(NA_NCB, NA_QCOLS)
    q_ws = np.clip(qc - NA_COLS // 2, 0, GRID_W - NA_COLS)
    kc3 = kc_idx[:, None, :]
    col_ok = (kc3 >= q_ws[..., None]) & (kc3 < q_ws[..., None] + NA_COLS)
    col_off = np.clip(kc3 - qc[..., None] + NA_COLS - 1, 0, 2 * NA_COLS - 2)
    scale = hd ** -0.5

    kg = k.reshape(b, rows, GRID_W, h, hd)
    vg = v.reshape(b, rows, GRID_W, h, hd)
    q_rows = q.reshape(b, rows, NA_NCB, NA_QCOLS, h, hd).swapaxes(0, 1)

    def row_fn(args):
        q_r, r, rs = args
        k_blk = lax.dynamic_slice_in_dim(kg, rs, kr, axis=1)[:, :, kc_idx]
        v_blk = lax.dynamic_slice_in_dim(vg, rs, kr, axis=1)[:, :, kc_idx]
        s = jnp.einsum('bcqhd,bicjhd->bhcqij', q_r, k_blk).astype(jnp.float32) * scale
        roff = rs - r + jnp.arange(kr) + NA_ROWS - 1
        bias = rpb[:, roff][:, :, col_off].transpose(0, 2, 3, 1, 4)
        s = jnp.where(col_ok[:, :, None, :], s + bias.astype(jnp.float32), NEG_INF)
        p = jax.nn.softmax(s.reshape(s.shape[:4] + (kr * NA_KCOLS,)), axis=-1).reshape(s.shape)
        return jnp.einsum('bhcqij,bicjhd->bcqhd', p.astype(v.dtype), v_blk)

    o = lax.map(row_fn, (q_rows, jnp.arange(rows, dtype=jnp.int32), jnp.asarray(row_start)))
    return o.swapaxes(0, 1).reshape(b, s_len, h, hd)


def mla_attention(q_nope, q_rope, k_nope, k_rope, v):
    b, s_len, h, _ = q_nope.shape
    nb = s_len // MLA_Q_BLK
    scale = (MLA_NOPE + MLA_ROPE) ** -0.5

    def blocks(t):
        return t.reshape((b, nb, MLA_Q_BLK) + t.shape[2:]).swapaxes(0, 1)

    def attend(qs):
        qn, qr = qs
        s = (jnp.einsum('bqhd,bkhd->bhqk', qn, k_nope)
             + jnp.einsum('bqhd,bkd->bhqk', qr, k_rope)).astype(jnp.float32) * scale
        p = jax.nn.softmax(s, axis=-1)
        return jnp.einsum('bhqk,bkhd->bqhd', p.astype(v.dtype), v)

    o = lax.map(attend, (blocks(q_nope), blocks(q_rope)))
    return o.swapaxes(0, 1).reshape(b, s_len, h, MLA_V)


def depthwise_conv3(u, w, bias):
    up = jnp.pad(u, ((0, 0), (1, 1), (0, 0)))
    return up[:, :-2] * w[0] + up[:, 1:-1] * w[1] + up[:, 2:] * w[2] + bias


def encoder_layer(x, attn_norm, w_in, na_rpb, sink, mla_q_norm, w_uq, mla_kv_norm, w_ukv, w_out,
                  ffn_norm, w_up, conv_w, conv_b, w_down):
    b, s_len, _ = x.shape
    h = rms_norm(x, attn_norm)
    proj = h @ w_in
    qa, ka, va, qb, kb, vb, qc, kc, vc, cq, ckv, k_rope_in = jnp.split(proj, SPLIT_POINTS, axis=-1)

    def heads(t, n):
        return t.reshape(b, s_len, n, -1)

    oa = neighborhood_attention(heads(qa, NA_HEADS), heads(ka, NA_HEADS), heads(va, NA_HEADS), na_rpb)

    qb = rotary(heads(qb, SW_HEADS)).reshape(b, s_len, SW_KV_HEADS, SW_GROUP, HEAD_DIM)
    kb = rotary(heads(kb, SW_KV_HEADS))
    ob, _ = banded_attention(qb, kb, heads(vb, SW_KV_HEADS), SW_WINDOW,
                             sink.reshape(SW_KV_HEADS, SW_GROUP))

    qc = rotary(heads(qc, DIL_HEADS))
    kc = rotary(heads(kc, DIL_HEADS))
    vc = heads(vc, DIL_HEADS)
    outs, lses = [], []
    for gi, (win, dil) in enumerate(DIL_PAIRS):
        sl = slice(gi * DIL_HEADS_PER_GROUP, (gi + 1) * DIL_HEADS_PER_GROUP)
        o_g, lse_g = dilated_attention(qc[:, :, sl], kc[:, :, sl], vc[:, :, sl], win // 2, dil)
        outs.append(o_g)
        lses.append(lse_g)
    alpha = jax.nn.softmax(jnp.stack(lses, axis=0), axis=0)
    oc = jnp.concatenate([(o_g.astype(jnp.float32) * alpha[gi][..., None]).astype(x.dtype)
                          for gi, o_g in enumerate(outs)], axis=2)

    q_full = (rms_norm(cq, mla_q_norm) @ w_uq).reshape(b, s_len, MLA_HEADS, MLA_NOPE + MLA_ROPE)
    q_nope, q_rope = q_full[..., :MLA_NOPE], rotary(q_full[..., MLA_NOPE:])
    kv = (rms_norm(ckv, mla_kv_norm) @ w_ukv).reshape(b, s_len, MLA_HEADS, MLA_NOPE + MLA_V)
    k_nope, v_d = kv[..., :MLA_NOPE], kv[..., MLA_NOPE:]
    od = mla_attention(q_nope, q_rope, k_nope, rotary(k_rope_in), v_d)

    mix = jnp.concatenate([oa.reshape(b, s_len, -1), ob.reshape(b, s_len, -1),
                           oc.reshape(b, s_len, -1), od.reshape(b, s_len, -1)], axis=-1)
    x = x + mix @ w_out

    h = rms_norm(x, ffn_norm)
    u = depthwise_conv3(h @ w_up, conv_w, conv_b)
    gate, up = jnp.split(u, 2, axis=-1)
    return x + (jax.nn.silu(gate) * up) @ w_down


def trunk(x, attn_norm, w_in, na_rpb, sink, mla_q_norm, w_uq, mla_kv_norm, w_ukv, w_out,
          ffn_norm, w_up, conv_w, conv_b, w_down, final_norm):
    for l in range(DEPTH):
        x = encoder_layer(x, attn_norm[l], w_in[l], na_rpb[l], sink[l], mla_q_norm[l], w_uq[l],
                          mla_kv_norm[l], w_ukv[l], w_out[l], ffn_norm[l], w_up[l], conv_w[l],
                          conv_b[l], w_down[l])
    return rms_norm(x, final_norm)


def setup_inputs(seed: int = 0) -> dict:
    key = jax.random.key(seed)
    ks = jax.random.split(key, 18)

    def nrm(k, shape, scale):
        return jax.random.normal(k, shape, jnp.float32) * scale

    return {
        "x_prompt": nrm(ks[0], (BATCH, SEQ, D_MODEL), 1.0),
        "x_sample": nrm(ks[1], (DEC_BATCH, DEC_SEQ, D_MODEL), 1.0),
        "attn_norm": 1.0 + nrm(ks[2], (DEPTH, D_MODEL), 0.05),
        "w_in": nrm(ks[3], (DEPTH, D_MODEL, IN_W), D_MODEL ** -0.5),
        "na_rpb": nrm(ks[4], (DEPTH, NA_HEADS, 2 * NA_ROWS - 1, 2 * NA_COLS - 1), 0.5),
        "sink": nrm(ks[5], (DEPTH, SW_HEADS), 1.0),
        "mla_q_norm": 1.0 + nrm(ks[6], (DEPTH, MLA_Q_RANK), 0.05),
        "w_uq": nrm(ks[7], (DEPTH, MLA_Q_RANK, MLA_HEADS * (MLA_NOPE + MLA_ROPE)), MLA_Q_RANK ** -0.5),
        "mla_kv_norm": 1.0 + nrm(ks[8], (DEPTH, MLA_KV_RANK), 0.05),
        "w_ukv": nrm(ks[9], (DEPTH, MLA_KV_RANK, MLA_HEADS * (MLA_NOPE + MLA_V)), MLA_KV_RANK ** -0.5),
        "w_out": nrm(ks[10], (DEPTH, MIX_W, D_MODEL), MIX_W ** -0.5),
        "ffn_norm": 1.0 + nrm(ks[11], (DEPTH, D_MODEL), 0.05),
        "w_up": nrm(ks[12], (DEPTH, D_MODEL, 2 * D_FF), D_MODEL ** -0.5),
        "conv_w": nrm(ks[13], (DEPTH, 3, 2 * D_FF), 3 ** -0.5),
        "conv_b": nrm(ks[14], (DEPTH, 2 * D_FF), 0.02),
        "w_down": nrm(ks[15], (DEPTH, D_FF, D_MODEL), D_FF ** -0.5),
        "final_norm": 1.0 + nrm(ks[16], (D_MODEL,), 0.05),
    }


def reference(x_prompt, x_sample, attn_norm, w_in, na_rpb, sink, mla_q_norm, w_uq, mla_kv_norm, w_ukv,
              w_out, ffn_norm, w_up, conv_w, conv_b, w_down, final_norm):
    y_prompt = trunk(x_prompt, attn_norm, w_in, na_rpb, sink, mla_q_norm, w_uq, mla_kv_norm, w_ukv,
                     w_out, ffn_norm, w_up, conv_w, conv_b, w_down, final_norm)
    y_sample = trunk(x_sample, attn_norm, w_in, na_rpb, sink, mla_q_norm, w_uq, mla_kv_norm, w_ukv,
                     w_out, ffn_norm, w_up, conv_w, conv_b, w_down, final_norm)
    return (y_prompt, y_sample)
```

```python
import functools

import numpy as np
import jax
import jax.numpy as jnp
from jax import lax
from jax.experimental import pallas as pl
from jax.experimental.pallas import tpu as pltpu

F32 = jnp.float32
BF16 = jnp.bfloat16

HEAD_DIM = 128
ROPE_THETA = 10000.0
NORM_EPS = 1e-6
NEG_INF = -1e30
GRID_W = 64
NA_HEADS = 8
NA_ROWS = 8
NA_COLS = 16
SW_HEADS = 8
SW_KV_HEADS = 2
SW_GROUP = SW_HEADS // SW_KV_HEADS
SW_WINDOW = 128
DIL_PAIRS = ((128, 1), (512, 4), (2048, 16))
DIL_HPG = 3
DIL_HEADS = DIL_HPG * len(DIL_PAIRS)
MLA_HEADS = 7
MLA_Q_RANK = 896
MLA_KV_RANK = 320
MLA_NOPE = 128
MLA_ROPE = 64
MLA_V = 128

A_W = NA_HEADS * HEAD_DIM
B_QW = SW_HEADS * HEAD_DIM
B_KW = SW_KV_HEADS * HEAD_DIM
C_W = DIL_HEADS * HEAD_DIM
D_OUT_W = MLA_HEADS * MLA_V
MAIN_W = 3 * A_W + B_QW + 2 * B_KW + 3 * C_W

LANE = 128
MXU_TILE = 1024
VMEM_LIMIT = 56 * 1024 * 1024

QA_BLK, KA_BLK, VA_BLK = 0, A_W // LANE, 2 * A_W // LANE
QB_BLK = 3 * A_W // LANE
KB_BLK = QB_BLK + B_QW // LANE
VB_BLK = KB_BLK + B_KW // LANE
QC_BLK = VB_BLK + B_KW // LANE
KC_BLK = QC_BLK + C_W // LANE
VC_BLK = KC_BLK + C_W // LANE

KV_PAD = 384
MLA_IN_W = MLA_Q_RANK + KV_PAD + LANE
MLA_QK = 2 * LANE


def _round_up(x, m):
    return (x + m - 1) // m * m


def _params(sem, vmem=VMEM_LIMIT):
    return pltpu.CompilerParams(dimension_semantics=sem, vmem_limit_bytes=vmem)


def _rms_kernel(x_ref, g_ref, o_ref):
    x = x_ref[...]
    ms = jnp.mean(x * x, axis=-1, keepdims=True)
    o_ref[...] = ((x * lax.rsqrt(ms + NORM_EPS)) * g_ref[...]).astype(o_ref.dtype)


def rms_rows(x, g, out_dtype):
    t, d = x.shape
    tm = min(256, t)
    return pl.pallas_call(
        _rms_kernel,
        out_shape=jax.ShapeDtypeStruct((t, d), out_dtype),
        grid=(t // tm,),
        in_specs=[pl.BlockSpec((tm, d), lambda i: (i, 0)), pl.BlockSpec((1, d), lambda i: (0, 0))],
        out_specs=pl.BlockSpec((tm, d), lambda i: (i, 0)),
        compiler_params=_params(("parallel",)),
        name="rms_rows",
    )(x, g.reshape(1, d))


def _mm_kernel(a_ref, b_ref, o_ref):
    o_ref[...] = jnp.dot(a_ref[...], b_ref[...], preferred_element_type=F32).astype(o_ref.dtype)


def _mm_res_kernel(a_ref, b_ref, r_ref, o_ref):
    o_ref[...] = r_ref[...] + jnp.dot(a_ref[...], b_ref[...], preferred_element_type=F32)


def _mm_res_acc_kernel(a_ref, b_ref, r_ref, o_ref, acc_ref):
    k = pl.program_id(2)

    @pl.when(k == 0)
    def _():
        acc_ref[...] = jnp.zeros_like(acc_ref)

    acc_ref[...] += jnp.dot(a_ref[...], b_ref[...], preferred_element_type=F32)

    @pl.when(k == pl.num_programs(2) - 1)
    def _():
        o_ref[...] = r_ref[...] + acc_ref[...]


def matmul(a, b, out_dtype, tm=MXU_TILE, tn=MXU_TILE):
    m, k = a.shape
    n = b.shape[1]
    tm, tn = min(tm, m), min(tn, n)
    return pl.pallas_call(
        _mm_kernel,
        out_shape=jax.ShapeDtypeStruct((m, n), out_dtype),
        grid=(m // tm, n // tn),
        in_specs=[pl.BlockSpec((tm, k), lambda i, j: (i, 0)), pl.BlockSpec((k, tn), lambda i, j: (0, j))],
        out_specs=pl.BlockSpec((tm, tn), lambda i, j: (i, j)),
        compiler_params=_params(("parallel", "parallel")),
        name="matmul",
    )(a, b)


def matmul_residual(a, b, res, tm=MXU_TILE, tn=MXU_TILE // 2, tk=None):
    m, k = a.shape
    n = b.shape[1]
    tm, tn = min(tm, m), min(tn, n)
    if tk is None or tk >= k:
        return pl.pallas_call(
            _mm_res_kernel,
            out_shape=jax.ShapeDtypeStruct((m, n), F32),
            grid=(m // tm, n // tn),
            in_specs=[
                pl.BlockSpec((tm, k), lambda i, j: (i, 0)),
                pl.BlockSpec((k, tn), lambda i, j: (0, j)),
                pl.BlockSpec((tm, tn), lambda i, j: (i, j)),
            ],
            out_specs=pl.BlockSpec((tm, tn), lambda i, j: (i, j)),
            compiler_params=_params(("parallel", "parallel")),
            name="matmul_residual",
        )(a, b, res)
    return pl.pallas_call(
        _mm_res_acc_kernel,
        out_shape=jax.ShapeDtypeStruct((m, n), F32),
        grid=(m // tm, n // tn, k // tk),
        in_specs=[
            pl.BlockSpec((tm, tk), lambda i, j, kk: (i, kk)),
            pl.BlockSpec((tk, tn), lambda i, j, kk: (kk, j)),
            pl.BlockSpec((tm, tn), lambda i, j, kk: (i, j)),
        ],
        out_specs=pl.BlockSpec((tm, tn), lambda i, j, kk: (i, j)),
        scratch_shapes=[pltpu.VMEM((tm, tn), F32)],
        compiler_params=_params(("parallel", "parallel", "arbitrary")),
        name="matmul_residual_ksplit",
    )(a, b, res)


def _rope(x, cos, sin_signed):
    return x * cos + pltpu.roll(x, LANE // 2, 1) * sin_signed


def rope_tables(s_len, d):
    inv = 1.0 / (ROPE_THETA ** (jnp.arange(0, d, 2, dtype=F32) / d))
    ang = jnp.arange(s_len, dtype=F32)[:, None] * inv[None, :]
    cos, sin = jnp.cos(ang), jnp.sin(ang)
    pad = jnp.zeros((s_len, LANE // 2 - d // 2), F32)
    return (jnp.concatenate([cos, pad, cos, pad], axis=-1), jnp.concatenate([-sin, pad, sin, pad], axis=-1))


def _na_kernel(q_ref, k_ref, v_ref, bias_ref, o_ref, *, rows, kr):
    scale = HEAD_DIM**-0.5

    def body(r, carry):
        rs = jnp.clip(r - kr // 2, 0, rows - kr)
        q = q_ref[pl.ds(pl.multiple_of(r * GRID_W, GRID_W), GRID_W), :]
        k0 = pl.multiple_of(rs * GRID_W, GRID_W)
        kw = k_ref[pl.ds(k0, kr * GRID_W), :]
        vw = v_ref[pl.ds(k0, kr * GRID_W), :]
        s = lax.dot_general(q, kw, (((1,), (1,)), ((), ())), preferred_element_type=F32) * scale
        s = s + bias_ref[r - rs]
        m = jnp.max(s, axis=-1, keepdims=True)
        p = jnp.exp(s - m)
        l = jnp.sum(p, axis=-1, keepdims=True)
        o = jnp.dot(p.astype(BF16), vw, preferred_element_type=F32) / l
        o_ref[pl.ds(pl.multiple_of(r * GRID_W, GRID_W), GRID_W), :] = o.astype(o_ref.dtype)
        return carry

    lax.fori_loop(0, rows, body, 0)


def na_bias_table(rpb, kr):
    delta = np.arange(kr)[:, None, None, None]
    qc = np.arange(GRID_W)[None, :, None, None]
    i = np.arange(kr)[None, None, :, None]
    kc = np.arange(GRID_W)[None, None, None, :]
    roff = np.broadcast_to(i - delta + NA_ROWS - 1, (kr, GRID_W, kr, GRID_W))
    q_ws = np.clip(qc - NA_COLS // 2, 0, GRID_W - NA_COLS)
    valid = np.broadcast_to((kc >= q_ws) & (kc < q_ws + NA_COLS), roff.shape)
    coff = np.broadcast_to(np.clip(kc - qc + NA_COLS - 1, 0, 2 * NA_COLS - 2), roff.shape)
    roff = np.clip(roff, 0, 2 * NA_ROWS - 2)
    tab = rpb.astype(F32)[:, roff, coff]
    tab = jnp.where(valid[None], tab, NEG_INF)
    return tab.reshape(rpb.shape[0], kr, GRID_W, kr * GRID_W)


def na_attention(proj3, bias_tab):
    b, s_len, _ = proj3.shape
    rows = s_len // GRID_W
    kr = min(NA_ROWS, rows)
    blk = lambda off: pl.BlockSpec((None, s_len, LANE), lambda bi, h, off=off: (bi, 0, off + h))
    return pl.pallas_call(
        functools.partial(_na_kernel, rows=rows, kr=kr),
        out_shape=jax.ShapeDtypeStruct((b, s_len, A_W), BF16),
        grid=(b, NA_HEADS),
        in_specs=[
            blk(QA_BLK),
            blk(KA_BLK),
            blk(VA_BLK),
            pl.BlockSpec((None, kr, GRID_W, kr * GRID_W), lambda bi, h: (h, 0, 0, 0)),
        ],
        out_specs=pl.BlockSpec((None, s_len, LANE), lambda bi, h: (bi, 0, h)),
        compiler_params=_params(("parallel", "parallel")),
        name="na_attention",
    )(proj3, proj3, proj3, bias_tab)


def _band_kernel(*refs, sub, tq, hw, win, has_sink, want_lse):
    q_ref, k_ref, v_ref, cos_ref, sin_ref = refs[:5]
    pos = 5
    sink_ref = None
    if has_sink:
        sink_ref = refs[pos]
        pos += 1
    o_ref = refs[pos]
    pos += 1
    lse_ref = None
    if want_lse:
        lse_ref = refs[pos]
        pos += 1
    krot_ref = refs[pos]
    scale = HEAD_DIM**-0.5

    krot_ref[...] = _rope(k_ref[...].astype(F32), cos_ref[...], sin_ref[...]).astype(BF16)

    def body(i, carry):
        q0 = pl.multiple_of(i * tq, tq)
        ws = pl.multiple_of(jnp.clip(q0 - hw, 0, sub - win), 64)
        q = _rope(q_ref[pl.ds(q0, tq), :].astype(F32), cos_ref[pl.ds(q0, tq), :], sin_ref[pl.ds(q0, tq), :])
        kw = krot_ref[pl.ds(ws, win), :]
        vw = v_ref[pl.ds(ws, win), :]
        s = lax.dot_general(q.astype(BF16), kw, (((1,), (1,)), ((), ())), preferred_element_type=F32) * scale
        qpos = q0 + lax.broadcasted_iota(jnp.int32, (tq, win), 0)
        kpos = ws + lax.broadcasted_iota(jnp.int32, (tq, win), 1)
        s = jnp.where(jnp.abs(kpos - qpos) <= hw, s, NEG_INF)
        m = jnp.max(s, axis=-1, keepdims=True)
        if has_sink:
            sk = sink_ref[0:1, 0:1]
            m = jnp.maximum(m, sk)
        p = jnp.exp(s - m)
        l = jnp.sum(p, axis=-1, keepdims=True)
        if has_sink:
            l = l + jnp.exp(sk - m)
        o = jnp.dot(p.astype(BF16), vw, preferred_element_type=F32) / l
        o_ref[pl.ds(q0, tq), :] = o.astype(o_ref.dtype)
        if want_lse:
            lse_ref[pl.ds(q0, tq), :] = jnp.broadcast_to(m + jnp.log(l), (tq, LANE))
        return carry

    lax.fori_loop(0, sub // tq, body, 0)


def banded_attention(proj3, cos, sin, *, dil, hw, n_heads, q_blk, k_blk, v_blk, kv_group, sink=None, want_lse=False):
    b, s_len, p_w = proj3.shape
    sub = s_len // dil
    tq = min(128, sub)
    win = min(tq + 2 * hw, sub)
    pb = p_w // LANE
    view = proj3.reshape(b, sub, dil * p_w)
    cos_v = cos.reshape(sub, dil * LANE)
    sin_v = sin.reshape(sub, dil * LANE)
    out_w = n_heads * LANE

    def col(off, group=1):
        return pl.BlockSpec((None, sub, LANE), lambda bi, r, h: (bi, 0, r * pb + off + h // group))

    in_specs = [
        col(q_blk),
        col(k_blk, kv_group),
        col(v_blk, kv_group),
        pl.BlockSpec((sub, LANE), lambda bi, r, h: (0, r)),
        pl.BlockSpec((sub, LANE), lambda bi, r, h: (0, r)),
    ]
    args = [view, view, view, cos_v, sin_v]
    if sink is not None:
        in_specs.append(pl.BlockSpec((None, 8, LANE), lambda bi, r, h: (h, 0, 0)))
        args.append(jnp.broadcast_to(sink.astype(F32)[:, None, None], (n_heads, 8, LANE)))
    out_spec = pl.BlockSpec((None, sub, LANE), lambda bi, r, h: (bi, 0, r * n_heads + h))
    out_shape = [jax.ShapeDtypeStruct((b, sub, dil * out_w), BF16)]
    out_specs = [out_spec]
    if want_lse:
        out_shape.append(jax.ShapeDtypeStruct((b, sub, dil * out_w), F32))
        out_specs.append(out_spec)
    outs = pl.pallas_call(
        functools.partial(
            _band_kernel, sub=sub, tq=tq, hw=hw, win=win, has_sink=sink is not None, want_lse=want_lse
        ),
        out_shape=out_shape,
        grid=(b, dil, n_heads),
        in_specs=in_specs,
        out_specs=out_specs,
        scratch_shapes=[pltpu.VMEM((sub, LANE), BF16)],
        compiler_params=_params(("parallel", "parallel", "arbitrary")),
        name=f"banded_attention_d{dil}",
    )(*args)
    return [o.reshape(b, s_len, out_w) for o in outs]


def _merge_kernel(o0_ref, o1_ref, o2_ref, l0_ref, l1_ref, l2_ref, out_ref):
    l0, l1, l2 = l0_ref[...], l1_ref[...], l2_ref[...]
    mx = jnp.maximum(jnp.maximum(l0, l1), l2)
    e0, e1, e2 = jnp.exp(l0 - mx), jnp.exp(l1 - mx), jnp.exp(l2 - mx)
    den = e0 + e1 + e2
    gw = DIL_HPG * LANE
    for g, (o_ref, e) in enumerate(((o0_ref, e0), (o1_ref, e1), (o2_ref, e2))):
        out_ref[:, g * gw : (g + 1) * gw] = (o_ref[...].astype(F32) * (e / den)).astype(out_ref.dtype)


def merge_dilated(outs, lses):
    t, gw = outs[0].shape
    tm = min(512, t)
    spec = pl.BlockSpec((tm, gw), lambda i: (i, 0))
    return pl.pallas_call(
        _merge_kernel,
        out_shape=jax.ShapeDtypeStruct((t, len(outs) * gw), BF16),
        grid=(t // tm,),
        in_specs=[spec] * 6,
        out_specs=pl.BlockSpec((tm, len(outs) * gw), lambda i: (i, 0)),
        compiler_params=_params(("parallel",)),
        name="merge_dilated",
    )(*outs, *lses)


def _mla_pre_kernel(h_ref, w1_ref, gq_ref, gkv_ref, wuq_ref, wukv_ref, cos_ref, sin_ref, q_ref, kn_ref, v_ref, kr_ref):
    c = jnp.dot(h_ref[...], w1_ref[...], preferred_element_type=F32)
    cq = c[:, :MLA_Q_RANK]
    ckv = c[:, MLA_Q_RANK : MLA_Q_RANK + KV_PAD]
    kr = c[:, MLA_Q_RANK + KV_PAD :]
    cos, sin = cos_ref[...], sin_ref[...]

    nq = (cq * lax.rsqrt(jnp.mean(cq * cq, axis=-1, keepdims=True) + NORM_EPS)) * gq_ref[...]
    q = jnp.dot(nq.astype(BF16), wuq_ref[...], preferred_element_type=F32)
    for h in range(MLA_HEADS):
        lo = h * MLA_QK
        q_ref[:, lo : lo + LANE] = q[:, lo : lo + LANE].astype(q_ref.dtype)
        q_ref[:, lo + LANE : lo + MLA_QK] = _rope(q[:, lo + LANE : lo + MLA_QK], cos, sin).astype(q_ref.dtype)

    ms = jnp.sum(ckv * ckv, axis=-1, keepdims=True) * (1.0 / MLA_KV_RANK)
    nkv = (ckv * lax.rsqrt(ms + NORM_EPS)) * gkv_ref[...]
    kv = jnp.dot(nkv.astype(BF16), wukv_ref[...], preferred_element_type=F32)
    kn_ref[...] = kv[:, :D_OUT_W].astype(kn_ref.dtype)
    v_ref[...] = kv[:, D_OUT_W:].astype(v_ref.dtype)
    kr_ref[...] = _rope(kr, cos, sin).astype(kr_ref.dtype)


def mla_pre(h, w1, gq, gkv, wuq, wukv, cos, sin, s_len):
    t, d = h.shape
    tm = min(256, s_len)
    n_pos = s_len // tm
    const = lambda shape: pl.BlockSpec(shape, lambda i: (0, 0))
    row = lambda w: pl.BlockSpec((tm, w), lambda i: (i, 0))
    return pl.pallas_call(
        _mla_pre_kernel,
        out_shape=[
            jax.ShapeDtypeStruct((t, MLA_HEADS * MLA_QK), BF16),
            jax.ShapeDtypeStruct((t, D_OUT_W), BF16),
            jax.ShapeDtypeStruct((t, D_OUT_W), BF16),
            jax.ShapeDtypeStruct((t, LANE), BF16),
        ],
        grid=(t // tm,),
        in_specs=[
            row(d),
            const((d, MLA_IN_W)),
            const((1, MLA_Q_RANK)),
            const((1, KV_PAD)),
            const((MLA_Q_RANK, MLA_HEADS * MLA_QK)),
            const((KV_PAD, 2 * D_OUT_W)),
            pl.BlockSpec((tm, LANE), lambda i: (i % n_pos, 0)),
            pl.BlockSpec((tm, LANE), lambda i: (i % n_pos, 0)),
        ],
        out_specs=[row(MLA_HEADS * MLA_QK), row(D_OUT_W), row(D_OUT_W), row(LANE)],
        compiler_params=_params(("parallel",)),
        name="mla_pre",
    )(h, w1, gq, gkv, wuq, wukv, cos, sin)


def _mla_kernel(q_ref, kn_ref, kr_ref, v_ref, o_ref):
    scale = (MLA_NOPE + MLA_ROPE) ** -0.5
    nt = (((1,), (1,)), ((), ()))
    q = q_ref[...]
    s = lax.dot_general(q[:, :LANE], kn_ref[...], nt, preferred_element_type=F32)
    s = (s + lax.dot_general(q[:, LANE:], kr_ref[...], nt, preferred_element_type=F32)) * scale
    m = jnp.max(s, axis=-1, keepdims=True)
    p = jnp.exp(s - m)
    l = jnp.sum(p, axis=-1, keepdims=True)
    o_ref[...] = (jnp.dot(p.astype(BF16), v_ref[...], preferred_element_type=F32) / l).astype(o_ref.dtype)


def mla_attention(q, kn, kr, v):
    b, s_len, _ = q.shape
    tq = min(256, s_len)
    return pl.pallas_call(
        _mla_kernel,
        out_shape=jax.ShapeDtypeStruct((b, s_len, D_OUT_W), BF16),
        grid=(b, MLA_HEADS, s_len // tq),
        in_specs=[
            pl.BlockSpec((None, tq, MLA_QK), lambda bi, h, i: (bi, i, h)),
            pl.BlockSpec((None, s_len, LANE), lambda bi, h, i: (bi, 0, h)),
            pl.BlockSpec((None, s_len, LANE), lambda bi, h, i: (bi, 0, 0)),
            pl.BlockSpec((None, s_len, LANE), lambda bi, h, i: (bi, 0, h)),
        ],
        out_specs=pl.BlockSpec((None, tq, LANE), lambda bi, h, i: (bi, i, h)),
        compiler_params=_params(("parallel", "parallel", "arbitrary")),
        name="mla_attention",
    )(q, kn, kr, v)


def _conv_gate_kernel(g_ref, u_ref, wg_ref, wu_ref, bg_ref, bu_ref, o_ref):
    s_len = g_ref.shape[0]
    row = lax.broadcasted_iota(jnp.int32, g_ref.shape, 0)

    def conv(x_ref, w_ref, b_ref):
        x = x_ref[...].astype(F32)
        prev = jnp.where(row == 0, 0.0, pltpu.roll(x, 1, 0))
        nxt = jnp.where(row == s_len - 1, 0.0, pltpu.roll(x, s_len - 1, 0))
        w = w_ref[...]
        return prev * w[0:1] + x * w[1:2] + nxt * w[2:3] + b_ref[...]

    gate = conv(g_ref, wg_ref, bg_ref)
    up = conv(u_ref, wu_ref, bu_ref)
    o_ref[...] = ((gate / (1.0 + jnp.exp(-gate))) * up).astype(o_ref.dtype)


def conv_gate(u3, conv_w, conv_b):
    b, s_len, two_f = u3.shape
    fp = two_f // 2
    tc = LANE
    nb = fp // tc
    return pl.pallas_call(
        _conv_gate_kernel,
        out_shape=jax.ShapeDtypeStruct((b, s_len, fp), BF16),
        grid=(b, nb),
        in_specs=[
            pl.BlockSpec((None, s_len, tc), lambda bi, j: (bi, 0, j)),
            pl.BlockSpec((None, s_len, tc), lambda bi, j: (bi, 0, j + nb)),
            pl.BlockSpec((3, tc), lambda bi, j: (0, j)),
            pl.BlockSpec((3, tc), lambda bi, j: (0, j + nb)),
            pl.BlockSpec((1, tc), lambda bi, j: (0, j)),
            pl.BlockSpec((1, tc), lambda bi, j: (0, j + nb)),
        ],
        out_specs=pl.BlockSpec((None, s_len, tc), lambda bi, j: (bi, 0, j)),
        compiler_params=_params(("parallel", "parallel")),
        name="conv_gate",
    )(u3, u3, conv_w, conv_w, conv_b, conv_b)


def _pad_cols(w, width):
    return jnp.pad(w, ((0, 0),) * (w.ndim - 1) + ((0, width - w.shape[-1]),))


def _interleave_rope(w):
    z = jnp.zeros(w.shape[:-1] + (LANE // 2 - MLA_ROPE // 2,), w.dtype)
    return jnp.concatenate([w[..., : MLA_ROPE // 2], z, w[..., MLA_ROPE // 2 :], z], axis=-1)


def prepare_layer(w_in, na_rpb, w_uq, mla_kv_norm, w_ukv, w_out, w_up, conv_w, conv_b, w_down, kr_rows):
    d = w_in.shape[0]
    f = w_down.shape[0]
    fp = _round_up(f, MXU_TILE)
    main_w = _round_up(MAIN_W, MXU_TILE)
    w_main = _pad_cols(w_in[:, :MAIN_W], main_w).astype(BF16)
    o = MAIN_W
    w_mla = jnp.concatenate(
        [
            w_in[:, o : o + MLA_Q_RANK],
            _pad_cols(w_in[:, o + MLA_Q_RANK : o + MLA_Q_RANK + MLA_KV_RANK], KV_PAD),
            _interleave_rope(w_in[:, o + MLA_Q_RANK + MLA_KV_RANK :]),
        ],
        axis=-1,
    ).astype(BF16)
    uq = w_uq.reshape(MLA_Q_RANK, MLA_HEADS, MLA_NOPE + MLA_ROPE)
    wuq = jnp.concatenate([uq[..., :MLA_NOPE], _interleave_rope(uq[..., MLA_NOPE:])], axis=-1)
    wuq = wuq.reshape(MLA_Q_RANK, MLA_HEADS * MLA_QK).astype(BF16)
    ukv = w_ukv.reshape(MLA_KV_RANK, MLA_HEADS, MLA_NOPE + MLA_V)
    wukv = jnp.concatenate(
        [ukv[..., :MLA_NOPE].reshape(MLA_KV_RANK, D_OUT_W), ukv[..., MLA_NOPE:].reshape(MLA_KV_RANK, D_OUT_W)], axis=-1
    )
    wukv = jnp.pad(wukv, ((0, KV_PAD - MLA_KV_RANK), (0, 0))).astype(BF16)
    gkv = _pad_cols(mla_kv_norm.reshape(1, MLA_KV_RANK), KV_PAD)
    w_up_p = jnp.concatenate([_pad_cols(w_up[:, :f], fp), _pad_cols(w_up[:, f:], fp)], axis=-1).astype(BF16)
    conv_w_p = jnp.concatenate([_pad_cols(conv_w[:, :f], fp), _pad_cols(conv_w[:, f:], fp)], axis=-1)
    conv_b_p = jnp.concatenate([_pad_cols(conv_b[None, :f], fp), _pad_cols(conv_b[None, f:], fp)], axis=-1)
    w_down_p = jnp.pad(w_down, ((0, fp - f), (0, 0))).astype(BF16)
    return dict(
        w_main=w_main,
        w_mla=w_mla,
        wuq=wuq,
        wukv=wukv,
        gkv=gkv,
        w_out=w_out.astype(BF16),
        w_up=w_up_p,
        conv_w=conv_w_p,
        conv_b=conv_b_p,
        w_down=w_down_p,
        na_bias={kr: na_bias_table(na_rpb, kr) for kr in kr_rows},
    )


def encoder_layer(x, b, s_len, lw, attn_norm, sink, mla_q_norm, ffn_norm, tabs):
    t, d = x.shape
    cos128, sin128, cos64, sin64 = tabs
    h = rms_rows(x, attn_norm, BF16)
    proj = matmul(h, lw["w_main"], BF16)
    proj3 = proj.reshape(b, s_len, proj.shape[1])

    kr = min(NA_ROWS, s_len // GRID_W)
    oa = na_attention(proj3, lw["na_bias"][kr])

    (ob,) = banded_attention(
        proj3, cos128, sin128, dil=1, hw=SW_WINDOW, n_heads=SW_HEADS, q_blk=QB_BLK, k_blk=KB_BLK, v_blk=VB_BLK,
        kv_group=SW_GROUP, sink=sink,
    )

    outs, lses = [], []
    for gi, (win, dil) in enumerate(DIL_PAIRS):
        o_g, lse_g = banded_attention(
            proj3, cos128, sin128, dil=dil, hw=(win // 2) // dil, n_heads=DIL_HPG, q_blk=QC_BLK + gi * DIL_HPG,
            k_blk=KC_BLK + gi * DIL_HPG, v_blk=VC_BLK + gi * DIL_HPG, kv_group=1, want_lse=True,
        )
        outs.append(o_g.reshape(t, DIL_HPG * LANE))
        lses.append(lse_g.reshape(t, DIL_HPG * LANE))
    oc = merge_dilated(outs, lses)

    q, kn, v, krope = mla_pre(
        h, lw["w_mla"], mla_q_norm.reshape(1, MLA_Q_RANK), lw["gkv"], lw["wuq"], lw["wukv"], cos64, sin64, s_len
    )
    od = mla_attention(
        q.reshape(b, s_len, -1), kn.reshape(b, s_len, -1), krope.reshape(b, s_len, -1), v.reshape(b, s_len, -1)
    )

    mix = jnp.concatenate([oa.reshape(t, A_W), ob.reshape(t, B_QW), oc, od.reshape(t, D_OUT_W)], axis=-1)
    x = matmul_residual(mix, lw["w_out"], x)

    h2 = rms_rows(x, ffn_norm, BF16)
    u = matmul(h2, lw["w_up"], BF16)
    act = conv_gate(u.reshape(b, s_len, u.shape[1]), lw["conv_w"], lw["conv_b"])
    fp = act.shape[-1]
    tk = fp // 4 if (fp // 4) % LANE == 0 else fp
    return matmul_residual(act.reshape(t, fp), lw["w_down"], x, tn=MXU_TILE, tk=tk)


def trunk(x3, layers, attn_norm, sink, mla_q_norm, ffn_norm, final_norm):
    b, s_len, d = x3.shape
    tabs = rope_tables(s_len, HEAD_DIM) + rope_tables(s_len, MLA_ROPE)
    x = x3.reshape(b * s_len, d)
    for l, lw in enumerate(layers):
        x = encoder_layer(x, b, s_len, lw, attn_norm[l], sink[l], mla_q_norm[l], ffn_norm[l], tabs)
    return rms_rows(x, final_norm, F32).reshape(b, s_len, d)


def kernel(x_prompt, x_sample, attn_norm, w_in, na_rpb, sink, mla_q_norm, w_uq, mla_kv_norm, w_ukv, w_out, ffn_norm, w_up, conv_w, conv_b, w_down, final_norm):
    depth = w_in.shape[0]
    kr_rows = sorted({min(NA_ROWS, x.shape[1] // GRID_W) for x in (x_prompt, x_sample)})
    layers = [
        prepare_layer(
            w_in[l], na_rpb[l], w_uq[l], mla_kv_norm[l], w_ukv[l], w_out[l], w_up[l], conv_w[l], conv_b[l], w_down[l], kr_rows
        )
        for l in range(depth)
    ]
    y_prompt = trunk(x_prompt, layers, attn_norm, sink, mla_q_norm, ffn_norm, final_norm)
    y_sample = trunk(x_sample, layers, attn_norm, sink, mla_q_norm, ffn_norm, final_norm)
    return (y_prompt, y_sample)
```

```python
import functools

import numpy as np
import jax
import jax.numpy as jnp
from jax import lax
from jax.experimental import pallas as pl
from jax.experimental.pallas import tpu as pltpu

F32 = jnp.float32
BF16 = jnp.bfloat16

HEAD_DIM = 128
ROPE_THETA = 10000.0
NORM_EPS = 1e-6
NEG_INF = -1e30
GRID_W = 64
NA_HEADS = 8
NA_ROWS = 8
NA_COLS = 16
SW_HEADS = 8
SW_KV_HEADS = 2
SW_GROUP = SW_HEADS // SW_KV_HEADS
SW_WINDOW = 128
DIL_PAIRS = ((128, 1), (512, 4), (2048, 16))
DIL_HPG = 3
DIL_HEADS = DIL_HPG * len(DIL_PAIRS)
MLA_HEADS = 7
MLA_Q_RANK = 896
MLA_KV_RANK = 320
MLA_NOPE = 128
MLA_ROPE = 64
MLA_V = 128

A_W = NA_HEADS * HEAD_DIM
B_QW = SW_HEADS * HEAD_DIM
B_KW = SW_KV_HEADS * HEAD_DIM
C_W = DIL_HEADS * HEAD_DIM
D_OUT_W = MLA_HEADS * MLA_V
MAIN_W = 3 * A_W + B_QW + 2 * B_KW + 3 * C_W

LANE = 128
SUBLANE = 8
MXU_TILE = 1024
VMEM_LIMIT = 56 * 1024 * 1024

QA_BLK, KA_BLK, VA_BLK = 0, A_W // LANE, 2 * A_W // LANE
QB_BLK = 3 * A_W // LANE
KB_BLK = QB_BLK + B_QW // LANE
VB_BLK = KB_BLK + B_KW // LANE
QC_BLK = VB_BLK + B_KW // LANE
KC_BLK = QC_BLK + C_W // LANE
VC_BLK = KC_BLK + C_W // LANE

KV_PAD = 384
MLA_IN_W = MLA_Q_RANK + KV_PAD + LANE
MLA_QK = 2 * LANE

_NT = (((1,), (1,)), ((), ()))


def _round_up(x, m):
    return (x + m - 1) // m * m


def _params(sem, vmem=VMEM_LIMIT):
    return pltpu.CompilerParams(dimension_semantics=sem, vmem_limit_bytes=vmem)


def _rms_kernel(x_ref, g_ref, o_ref):
    x = x_ref[...]
    ms = jnp.mean(x * x, axis=-1, keepdims=True)
    o_ref[...] = ((x * lax.rsqrt(ms + NORM_EPS)) * g_ref[...]).astype(o_ref.dtype)


def rms_rows(x, g, out_dtype):
    t, d = x.shape
    tm = min(256, t)
    return pl.pallas_call(
        _rms_kernel,
        out_shape=jax.ShapeDtypeStruct((t, d), out_dtype),
        grid=(t // tm,),
        in_specs=[pl.BlockSpec((tm, d), lambda i: (i, 0)), pl.BlockSpec((1, d), lambda i: (0, 0))],
        out_specs=pl.BlockSpec((tm, d), lambda i: (i, 0)),
        compiler_params=_params(("parallel",)),
        name="rms_rows",
    )(x, g.reshape(1, d))


def _mm_kernel(a_ref, b_ref, o_ref):
    o_ref[...] = jnp.dot(a_ref[...], b_ref[...], preferred_element_type=F32).astype(o_ref.dtype)


def _mm_res_kernel(a_ref, b_ref, r_ref, o_ref):
    o_ref[...] = r_ref[...] + jnp.dot(a_ref[...], b_ref[...], preferred_element_type=F32)


def _mm_res_acc_kernel(a_ref, b_ref, r_ref, o_ref, acc_ref):
    k = pl.program_id(2)

    @pl.when(k == 0)
    def _():
        acc_ref[...] = jnp.zeros_like(acc_ref)

    acc_ref[...] += jnp.dot(a_ref[...], b_ref[...], preferred_element_type=F32)

    @pl.when(k == pl.num_programs(2) - 1)
    def _():
        o_ref[...] = r_ref[...] + acc_ref[...]


def matmul(a, b, out_dtype, tm=MXU_TILE, tn=MXU_TILE):
    m, k = a.shape
    n = b.shape[1]
    tm, tn = min(tm, m), min(tn, n)
    return pl.pallas_call(
        _mm_kernel,
        out_shape=jax.ShapeDtypeStruct((m, n), out_dtype),
        grid=(m // tm, n // tn),
        in_specs=[pl.BlockSpec((tm, k), lambda i, j: (i, 0)), pl.BlockSpec((k, tn), lambda i, j: (0, j))],
        out_specs=pl.BlockSpec((tm, tn), lambda i, j: (i, j)),
        compiler_params=_params(("parallel", "parallel")),
        name="matmul",
    )(a, b)


def matmul_residual(a, b, res, tm=MXU_TILE, tn=MXU_TILE // 2, tk=None):
    m, k = a.shape
    n = b.shape[1]
    tm, tn = min(tm, m), min(tn, n)
    if tk is None or tk >= k:
        return pl.pallas_call(
            _mm_res_kernel,
            out_shape=jax.ShapeDtypeStruct((m, n), F32),
            grid=(m // tm, n // tn),
            in_specs=[
                pl.BlockSpec((tm, k), lambda i, j: (i, 0)),
                pl.BlockSpec((k, tn), lambda i, j: (0, j)),
                pl.BlockSpec((tm, tn), lambda i, j: (i, j)),
            ],
            out_specs=pl.BlockSpec((tm, tn), lambda i, j: (i, j)),
            compiler_params=_params(("parallel", "parallel")),
            name="matmul_residual",
        )(a, b, res)
    return pl.pallas_call(
        _mm_res_acc_kernel,
        out_shape=jax.ShapeDtypeStruct((m, n), F32),
        grid=(m // tm, n // tn, k // tk),
        in_specs=[
            pl.BlockSpec((tm, tk), lambda i, j, kk: (i, kk)),
            pl.BlockSpec((tk, tn), lambda i, j, kk: (kk, j)),
            pl.BlockSpec((tm, tn), lambda i, j, kk: (i, j)),
        ],
        out_specs=pl.BlockSpec((tm, tn), lambda i, j, kk: (i, j)),
        scratch_shapes=[pltpu.VMEM((tm, tn), F32)],
        compiler_params=_params(("parallel", "parallel", "arbitrary")),
        name="matmul_residual_ksplit",
    )(a, b, res)


def _conv_gate_kernel(g_ref, u_ref, wg_ref, wu_ref, bg_ref, bu_ref, o_ref):
    s_len = g_ref.shape[0]
    row = lax.broadcasted_iota(jnp.int32, g_ref.shape, 0)

    def conv(x_ref, w_ref, b_ref):
        x = x_ref[...].astype(F32)
        prev = jnp.where(row == 0, 0.0, pltpu.roll(x, 1, 0))
        nxt = jnp.where(row == s_len - 1, 0.0, pltpu.roll(x, s_len - 1, 0))
        w = w_ref[...]
        return prev * w[0:1] + x * w[1:2] + nxt * w[2:3] + b_ref[...]

    gate = conv(g_ref, wg_ref, bg_ref)
    up = conv(u_ref, wu_ref, bu_ref)
    o_ref[...] = ((gate / (1.0 + jnp.exp(-gate))) * up).astype(o_ref.dtype)


def conv_gate(u3, conv_w, conv_b):
    b, s_len, two_f = u3.shape
    fp = two_f // 2
    tc = LANE
    nb = fp // tc
    return pl.pallas_call(
        _conv_gate_kernel,
        out_shape=jax.ShapeDtypeStruct((b, s_len, fp), BF16),
        grid=(b, nb),
        in_specs=[
            pl.BlockSpec((None, s_len, tc), lambda bi, j: (bi, 0, j)),
            pl.BlockSpec((None, s_len, tc), lambda bi, j: (bi, 0, j + nb)),
            pl.BlockSpec((3, tc), lambda bi, j: (0, j)),
            pl.BlockSpec((3, tc), lambda bi, j: (0, j + nb)),
            pl.BlockSpec((1, tc), lambda bi, j: (0, j)),
            pl.BlockSpec((1, tc), lambda bi, j: (0, j + nb)),
        ],
        out_specs=pl.BlockSpec((None, s_len, tc), lambda bi, j: (bi, 0, j)),
        compiler_params=_params(("parallel", "parallel")),
        name="conv_gate",
    )(u3, u3, conv_w, conv_w, conv_b, conv_b)


def _rope(x, cos, sin_signed):
    return x * cos + pltpu.roll(x, LANE // 2, 1) * sin_signed


def rope_tables(s_len, d):
    inv = 1.0 / (ROPE_THETA ** (jnp.arange(0, d, 2, dtype=F32) / d))
    ang = jnp.arange(s_len, dtype=F32)[:, None] * inv[None, :]
    cos, sin = jnp.cos(ang), jnp.sin(ang)
    pad = jnp.zeros((s_len, LANE // 2 - d // 2), F32)
    return (jnp.concatenate([cos, pad, cos, pad], axis=-1), jnp.concatenate([-sin, pad, sin, pad], axis=-1))


def _na_kernel(q_ref, k_ref, v_ref, bias_ref, o_ref, *, rows, kr, unroll):
    scale = HEAD_DIM**-0.5

    def one_row(r):
        rs = jnp.clip(r - kr // 2, 0, rows - kr)
        q = q_ref[pl.ds(pl.multiple_of(r * GRID_W, GRID_W), GRID_W), :]
        k0 = pl.multiple_of(rs * GRID_W, GRID_W)
        kw = k_ref[pl.ds(k0, kr * GRID_W), :]
        vw = v_ref[pl.ds(k0, kr * GRID_W), :]
        s = lax.dot_general(q, kw, _NT, preferred_element_type=F32) * scale
        s = s + bias_ref[r - rs]
        m = jnp.max(s, axis=-1, keepdims=True)
        p = jnp.exp(s - m)
        l = jnp.sum(p, axis=-1, keepdims=True)
        o = jnp.dot(p.astype(BF16), vw, preferred_element_type=F32) / l
        o_ref[pl.ds(pl.multiple_of(r * GRID_W, GRID_W), GRID_W), :] = o.astype(o_ref.dtype)

    def body(g, carry):
        for u in range(unroll):
            one_row(g * unroll + u)
        return carry

    lax.fori_loop(0, rows // unroll, body, 0)


def na_bias_tables(rpb_all, kr):
    qc = np.arange(GRID_W)[:, None]
    kc = np.arange(GRID_W)[None, :]
    q_ws = np.clip(qc - NA_COLS // 2, 0, GRID_W - NA_COLS)
    valid = (kc >= q_ws) & (kc < q_ws + NA_COLS)
    coff = np.clip(kc - qc + NA_COLS - 1, 0, 2 * NA_COLS - 2)
    onehot = (coff[None] == np.arange(2 * NA_COLS - 1)[:, None, None]).astype(np.float32)
    e = jnp.einsum("lhrc,cqk->lhrqk", rpb_all.astype(F32), onehot, precision=lax.Precision.HIGHEST)
    e = jnp.where(valid, e, NEG_INF)
    tab = jnp.stack([e[:, :, NA_ROWS - 1 - dl : NA_ROWS - 1 - dl + kr] for dl in range(kr)], axis=2)
    tab = tab.transpose(0, 1, 2, 4, 3, 5)
    return tab.reshape(tab.shape[:3] + (GRID_W, kr * GRID_W))


def na_attention(proj3, bias_tab):
    b, s_len, _ = proj3.shape
    rows = s_len // GRID_W
    kr = min(NA_ROWS, rows)
    blk = lambda off: pl.BlockSpec((None, s_len, LANE), lambda bi, h, off=off: (bi, 0, off + h))
    return pl.pallas_call(
        functools.partial(_na_kernel, rows=rows, kr=kr, unroll=8 if rows % 8 == 0 else 1),
        out_shape=jax.ShapeDtypeStruct((b, s_len, A_W), BF16),
        grid=(b, NA_HEADS),
        in_specs=[
            blk(QA_BLK),
            blk(KA_BLK),
            blk(VA_BLK),
            pl.BlockSpec((None, kr, GRID_W, kr * GRID_W), lambda bi, h: (h, 0, 0, 0)),
        ],
        out_specs=pl.BlockSpec((None, s_len, LANE), lambda bi, h: (bi, 0, h)),
        compiler_params=_params(("parallel", "parallel")),
        name="na_attention",
    )(proj3, proj3, proj3, bias_tab)


def _band_kernel(*refs, s_len, dil, tq, hw, win, has_sink, want_lse, unroll):
    refs = list(refs)
    q_ref, k_ref, v_ref, cos_ref, sin_ref = refs[:5]
    del refs[:5]
    sink_ref = refs.pop(0) if has_sink else None
    o_ref = refs.pop(0)
    lse_ref = refs.pop(0) if want_lse else None
    stage, qd, kd, vd = refs[:4]
    od = refs[4] if dil > 1 else None
    ld = refs[5] if (dil > 1 and want_lse) else None
    sub = s_len // dil
    nq = sub // tq
    scale = HEAD_DIM**-0.5

    def regroup(val, dst):
        if dil == 1:
            dst[0] = val.astype(BF16)
        else:
            stage[...] = val
            for r in range(dil):
                dst[r] = stage[pl.ds(r, sub, stride=dil), :].astype(BF16)

    cos, sin = cos_ref[...], sin_ref[...]
    regroup(_rope(k_ref[...].astype(F32), cos, sin), kd)
    regroup(_rope(q_ref[...].astype(F32), cos, sin), qd)
    regroup(v_ref[...].astype(F32), vd)

    def one_block(it):
        r = it // nq
        q0 = pl.multiple_of((it % nq) * tq, tq)
        ws = pl.multiple_of(jnp.clip(q0 - hw, 0, sub - win), 64)
        s = lax.dot_general(qd[r, pl.ds(q0, tq), :], kd[r, pl.ds(ws, win), :], _NT, preferred_element_type=F32) * scale
        qpos = q0 + lax.broadcasted_iota(jnp.int32, (tq, win), 0)
        kpos = ws + lax.broadcasted_iota(jnp.int32, (tq, win), 1)
        s = jnp.where(jnp.abs(kpos - qpos) <= hw, s, NEG_INF)
        m = jnp.max(s, axis=-1, keepdims=True)
        if has_sink:
            sk = sink_ref[0:1, 0:1]
            m = jnp.maximum(m, sk)
        p = jnp.exp(s - m)
        l = jnp.sum(p, axis=-1, keepdims=True)
        if has_sink:
            l = l + jnp.exp(sk - m)
        o = jnp.dot(p.astype(BF16), vd[r, pl.ds(ws, win), :], preferred_element_type=F32) / l
        if dil == 1:
            o_ref[pl.ds(q0, tq), :] = o.astype(o_ref.dtype)
        else:
            od[r, pl.ds(q0, tq), :] = o
        if want_lse:
            lse = jnp.broadcast_to(m + jnp.log(l), (tq, LANE))
            if dil == 1:
                lse_ref[pl.ds(q0, tq), :] = lse
            else:
                ld[r, pl.ds(q0, tq), :] = lse

    def body(g, carry):
        for u in range(unroll):
            one_block(g * unroll + u)
        return carry

    lax.fori_loop(0, dil * nq // unroll, body, 0)

    if dil > 1:
        for r in range(dil):
            stage[pl.ds(r, sub, stride=dil), :] = od[r]
        o_ref[...] = stage[...].astype(o_ref.dtype)
        if want_lse:
            for r in range(dil):
                lse_ref[pl.ds(r, sub, stride=dil), :] = ld[r]


def banded_attention(proj3, cos, sin, *, dil, hw, n_heads, q_blk, k_blk, v_blk, kv_group, sink=None, want_lse=False):
    b, s_len, _ = proj3.shape
    sub = s_len // dil
    tq = min(128, sub)
    win = min(tq + 2 * hw, sub)
    total = dil * (sub // tq)
    unroll = 4 if total % 4 == 0 else 1
    out_w = n_heads * LANE

    def col(off, group=1):
        return pl.BlockSpec((None, s_len, LANE), lambda bi, h: (bi, 0, off + h // group))

    table = pl.BlockSpec((s_len, LANE), lambda bi, h: (0, 0))
    in_specs = [col(q_blk), col(k_blk, kv_group), col(v_blk, kv_group), table, table]
    args = [proj3, proj3, proj3, cos, sin]
    if sink is not None:
        in_specs.append(pl.BlockSpec((None, SUBLANE, LANE), lambda bi, h: (h, 0, 0)))
        args.append(jnp.broadcast_to(sink.astype(F32)[:, None, None], (n_heads, SUBLANE, LANE)))
    out_spec = pl.BlockSpec((None, s_len, LANE), lambda bi, h: (bi, 0, h))
    out_shape = [jax.ShapeDtypeStruct((b, s_len, out_w), BF16)]
    out_specs = [out_spec]
    if want_lse:
        out_shape.append(jax.ShapeDtypeStruct((b, s_len, out_w), F32))
        out_specs.append(out_spec)
    slab = (dil, sub, LANE)
    scratch = [pltpu.VMEM((s_len, LANE), F32), pltpu.VMEM(slab, BF16), pltpu.VMEM(slab, BF16), pltpu.VMEM(slab, BF16)]
    if dil > 1:
        scratch.append(pltpu.VMEM(slab, F32))
        if want_lse:
            scratch.append(pltpu.VMEM(slab, F32))
    return pl.pallas_call(
        functools.partial(
            _band_kernel, s_len=s_len, dil=dil, tq=tq, hw=hw, win=win, has_sink=sink is not None,
            want_lse=want_lse, unroll=unroll,
        ),
        out_shape=out_shape,
        grid=(b, n_heads),
        in_specs=in_specs,
        out_specs=out_specs,
        scratch_shapes=scratch,
        compiler_params=_params(("parallel", "arbitrary")),
        name=f"banded_attention_d{dil}",
    )(*args)


def _merge_kernel(o0_ref, o1_ref, o2_ref, l0_ref, l1_ref, l2_ref, out_ref):
    l0, l1, l2 = l0_ref[...], l1_ref[...], l2_ref[...]
    mx = jnp.maximum(jnp.maximum(l0, l1), l2)
    e0, e1, e2 = jnp.exp(l0 - mx), jnp.exp(l1 - mx), jnp.exp(l2 - mx)
    den = e0 + e1 + e2
    gw = DIL_HPG * LANE
    for g, (o_ref, e) in enumerate(((o0_ref, e0), (o1_ref, e1), (o2_ref, e2))):
        out_ref[:, g * gw : (g + 1) * gw] = (o_ref[...].astype(F32) * (e / den)).astype(out_ref.dtype)


def merge_dilated(outs, lses):
    t, gw = outs[0].shape
    tm = min(512, t)
    spec = pl.BlockSpec((tm, gw), lambda i: (i, 0))
    return pl.pallas_call(
        _merge_kernel,
        out_shape=jax.ShapeDtypeStruct((t, len(outs) * gw), BF16),
        grid=(t // tm,),
        in_specs=[spec] * 6,
        out_specs=pl.BlockSpec((tm, len(outs) * gw), lambda i: (i, 0)),
        compiler_params=_params(("parallel",)),
        name="merge_dilated",
    )(*outs, *lses)


def _mla_pre_kernel(h_ref, w1_ref, gq_ref, gkv_ref, wuq_ref, wukv_ref, cos_ref, sin_ref, q_ref, k_ref, v_ref):
    c = jnp.dot(h_ref[...], w1_ref[...], preferred_element_type=F32)
    cq = c[:, :MLA_Q_RANK]
    ckv = c[:, MLA_Q_RANK : MLA_Q_RANK + KV_PAD]
    kr = c[:, MLA_Q_RANK + KV_PAD :]
    cos, sin = cos_ref[...], sin_ref[...]

    nq = (cq * lax.rsqrt(jnp.mean(cq * cq, axis=-1, keepdims=True) + NORM_EPS)) * gq_ref[...]
    q = jnp.dot(nq.astype(BF16), wuq_ref[...], preferred_element_type=F32)
    ms = jnp.sum(ckv * ckv, axis=-1, keepdims=True) * (1.0 / MLA_KV_RANK)
    nkv = (ckv * lax.rsqrt(ms + NORM_EPS)) * gkv_ref[...]
    kv = jnp.dot(nkv.astype(BF16), wukv_ref[...], preferred_element_type=F32)
    k_rope = _rope(kr, cos, sin).astype(k_ref.dtype)
    for h in range(MLA_HEADS):
        lo = h * MLA_QK
        q_ref[:, lo : lo + LANE] = q[:, lo : lo + LANE].astype(q_ref.dtype)
        q_ref[:, lo + LANE : lo + MLA_QK] = _rope(q[:, lo + LANE : lo + MLA_QK], cos, sin).astype(q_ref.dtype)
        k_ref[:, lo : lo + LANE] = kv[:, h * LANE : (h + 1) * LANE].astype(k_ref.dtype)
        k_ref[:, lo + LANE : lo + MLA_QK] = k_rope
    v_ref[...] = kv[:, D_OUT_W:].astype(v_ref.dtype)


def mla_pre(h, w1, gq, gkv, wuq, wukv, cos, sin, s_len):
    t, d = h.shape
    tm = min(256, s_len)
    n_pos = s_len // tm
    const = lambda shape: pl.BlockSpec(shape, lambda i: (0, 0))
    row = lambda w: pl.BlockSpec((tm, w), lambda i: (i, 0))
    return pl.pallas_call(
        _mla_pre_kernel,
        out_shape=[
            jax.ShapeDtypeStruct((t, MLA_HEADS * MLA_QK), BF16),
            jax.ShapeDtypeStruct((t, MLA_HEADS * MLA_QK), BF16),
            jax.ShapeDtypeStruct((t, D_OUT_W), BF16),
        ],
        grid=(t // tm,),
        in_specs=[
            row(d),
            const((d, MLA_IN_W)),
            const((1, MLA_Q_RANK)),
            const((1, KV_PAD)),
            const((MLA_Q_RANK, MLA_HEADS * MLA_QK)),
            const((KV_PAD, 2 * D_OUT_W)),
            pl.BlockSpec((tm, LANE), lambda i: (i % n_pos, 0)),
            pl.BlockSpec((tm, LANE), lambda i: (i % n_pos, 0)),
        ],
        out_specs=[row(MLA_HEADS * MLA_QK), row(MLA_HEADS * MLA_QK), row(D_OUT_W)],
        compiler_params=_params(("parallel",)),
        name="mla_pre",
    )(h, w1, gq, gkv, wuq, wukv, cos, sin)


def _mla_kernel(q_ref, k_ref, v_ref, o_ref, *, parts):
    scale = (MLA_NOPE + MLA_ROPE) ** -0.5
    rows = q_ref.shape[0] // parts
    for a in range(parts):
        sl = slice(a * rows, (a + 1) * rows)
        s = lax.dot_general(q_ref[sl, :], k_ref[...], _NT, preferred_element_type=F32) * scale
        m = jnp.max(s, axis=-1, keepdims=True)
        p = jnp.exp(s - m)
        l = jnp.sum(p, axis=-1, keepdims=True)
        o_ref[sl, :] = (jnp.dot(p.astype(BF16), v_ref[...], preferred_element_type=F32) / l).astype(o_ref.dtype)


def mla_attention(q, k, v):
    b, s_len, _ = q.shape
    tq = min(256, s_len)
    return pl.pallas_call(
        functools.partial(_mla_kernel, parts=2),
        out_shape=jax.ShapeDtypeStruct((b, s_len, D_OUT_W), BF16),
        grid=(b, MLA_HEADS, s_len // tq),
        in_specs=[
            pl.BlockSpec((None, tq, MLA_QK), lambda bi, h, i: (bi, i, h)),
            pl.BlockSpec((None, s_len, MLA_QK), lambda bi, h, i: (bi, 0, h)),
            pl.BlockSpec((None, s_len, LANE), lambda bi, h, i: (bi, 0, h)),
        ],
        out_specs=pl.BlockSpec((None, tq, LANE), lambda bi, h, i: (bi, i, h)),
        compiler_params=_params(("parallel", "parallel", "arbitrary")),
        name="mla_attention",
    )(q, k, v)


def _pad_cols(w, width):
    return jnp.pad(w, ((0, 0),) * (w.ndim - 1) + ((0, width - w.shape[-1]),))


def _interleave_rope(w):
    z = jnp.zeros(w.shape[:-1] + (LANE // 2 - MLA_ROPE // 2,), w.dtype)
    return jnp.concatenate([w[..., : MLA_ROPE // 2], z, w[..., MLA_ROPE // 2 :], z], axis=-1)


def _gate_up_blocks(w, f, fp):
    return jnp.concatenate([_pad_cols(w[..., :f], fp), _pad_cols(w[..., f:], fp)], axis=-1)


def prepare_layer(w_in, w_uq, mla_kv_norm, w_ukv, w_out, w_up, conv_w, conv_b, w_down):
    f = w_down.shape[0]
    fp = _round_up(f, MXU_TILE)
    main_w = _round_up(MAIN_W, MXU_TILE)
    w_main = _pad_cols(w_in[:, :MAIN_W], main_w).astype(BF16)
    o = MAIN_W
    w_mla = jnp.concatenate(
        [
            w_in[:, o : o + MLA_Q_RANK],
            _pad_cols(w_in[:, o + MLA_Q_RANK : o + MLA_Q_RANK + MLA_KV_RANK], KV_PAD),
            _interleave_rope(w_in[:, o + MLA_Q_RANK + MLA_KV_RANK :]),
        ],
        axis=-1,
    ).astype(BF16)
    uq = w_uq.reshape(MLA_Q_RANK, MLA_HEADS, MLA_NOPE + MLA_ROPE)
    wuq = jnp.concatenate([uq[..., :MLA_NOPE], _interleave_rope(uq[..., MLA_NOPE:])], axis=-1)
    wuq = wuq.reshape(MLA_Q_RANK, MLA_HEADS * MLA_QK).astype(BF16)
    ukv = w_ukv.reshape(MLA_KV_RANK, MLA_HEADS, MLA_NOPE + MLA_V)
    wukv = jnp.concatenate(
        [ukv[..., :MLA_NOPE].reshape(MLA_KV_RANK, D_OUT_W), ukv[..., MLA_NOPE:].reshape(MLA_KV_RANK, D_OUT_W)], axis=-1
    )
    wukv = jnp.pad(wukv, ((0, KV_PAD - MLA_KV_RANK), (0, 0))).astype(BF16)
    gkv = _pad_cols(mla_kv_norm.reshape(1, MLA_KV_RANK), KV_PAD)
    return dict(
        w_main=w_main,
        w_mla=w_mla,
        wuq=wuq,
        wukv=wukv,
        gkv=gkv,
        w_out=w_out.astype(BF16),
        w_up=_gate_up_blocks(w_up, f, fp).astype(BF16),
        conv_w=_gate_up_blocks(conv_w, f, fp),
        conv_b=_gate_up_blocks(conv_b[None], f, fp),
        w_down=jnp.pad(w_down, ((0, fp - f), (0, 0))).astype(BF16),
    )


def encoder_layer(x, b, s_len, lw, na_bias, attn_norm, sink, mla_q_norm, ffn_norm, tabs):
    t, d = x.shape
    cos128, sin128, cos64, sin64 = tabs
    h = rms_rows(x, attn_norm, BF16)
    proj = matmul(h, lw["w_main"], BF16)
    proj3 = proj.reshape(b, s_len, proj.shape[1])

    oa = na_attention(proj3, na_bias)

    (ob,) = banded_attention(
        proj3, cos128, sin128, dil=1, hw=SW_WINDOW, n_heads=SW_HEADS, q_blk=QB_BLK, k_blk=KB_BLK, v_blk=VB_BLK,
        kv_group=SW_GROUP, sink=sink,
    )

    outs, lses = [], []
    for gi, (win, dil) in enumerate(DIL_PAIRS):
        o_g, lse_g = banded_attention(
            proj3, cos128, sin128, dil=dil, hw=(win // 2) // dil, n_heads=DIL_HPG, q_blk=QC_BLK + gi * DIL_HPG,
            k_blk=KC_BLK + gi * DIL_HPG, v_blk=VC_BLK + gi * DIL_HPG, kv_group=1, want_lse=True,
        )
        outs.append(o_g.reshape(t, DIL_HPG * LANE))
        lses.append(lse_g.reshape(t, DIL_HPG * LANE))
    oc = merge_dilated(outs, lses)

    q, k, v = mla_pre(
        h, lw["w_mla"], mla_q_norm.reshape(1, MLA_Q_RANK), lw["gkv"], lw["wuq"], lw["wukv"], cos64, sin64, s_len
    )
    od = mla_attention(q.reshape(b, s_len, -1), k.reshape(b, s_len, -1), v.reshape(b, s_len, -1))

    mix = jnp.concatenate([oa.reshape(t, A_W), ob.reshape(t, B_QW), oc, od.reshape(t, D_OUT_W)], axis=-1)
    x = matmul_residual(mix, lw["w_out"], x)

    h2 = rms_rows(x, ffn_norm, BF16)
    u = matmul(h2, lw["w_up"], BF16)
    act = conv_gate(u.reshape(b, s_len, u.shape[1]), lw["conv_w"], lw["conv_b"])
    fp = act.shape[-1]
    tk = fp // 4 if (fp // 4) % LANE == 0 else fp
    return matmul_residual(act.reshape(t, fp), lw["w_down"], x, tn=MXU_TILE, tk=tk)


def trunk(x3, layers, na_bias, attn_norm, sink, mla_q_norm, ffn_norm, final_norm):
    b, s_len, d = x3.shape
    tabs = rope_tables(s_len, HEAD_DIM) + rope_tables(s_len, MLA_ROPE)
    x = x3.reshape(b * s_len, d)
    for l, lw in enumerate(layers):
        x = encoder_layer(x, b, s_len, lw, na_bias[l], attn_norm[l], sink[l], mla_q_norm[l], ffn_norm[l], tabs)
    return rms_rows(x, final_norm, F32).reshape(b, s_len, d)


def kernel(x_prompt, x_sample, attn_norm, w_in, na_rpb, sink, mla_q_norm, w_uq, mla_kv_norm, w_ukv, w_out, ffn_norm, w_up, conv_w, conv_b, w_down, final_norm):
    depth = w_in.shape[0]
    layers = [
        prepare_layer(w_in[l], w_uq[l], mla_kv_norm[l], w_ukv[l], w_out[l], w_up[l], conv_w[l], conv_b[l], w_down[l])
        for l in range(depth)
    ]
    outs = []
    bias_by_kr = {}
    for x3 in (x_prompt, x_sample):
        kr = min(NA_ROWS, x3.shape[1] // GRID_W)
        if kr not in bias_by_kr:
            bias_by_kr[kr] = na_bias_tables(na_rpb, kr)
        outs.append(trunk(x3, layers, bias_by_kr[kr], attn_norm, sink, mla_q_norm, ffn_norm, final_norm))
    return tuple(outs)
```

```python
import functools

import numpy as np
import jax
import jax.numpy as jnp
from jax import lax
from jax.experimental import pallas as pl
from jax.experimental.pallas import tpu as pltpu

F32 = jnp.float32
BF16 = jnp.bfloat16

HEAD_DIM = 128
ROPE_THETA = 10000.0
NORM_EPS = 1e-6
NEG_INF = -1e30
GRID_W = 64
NA_HEADS = 8
NA_ROWS = 8
NA_COLS = 16
SW_HEADS = 8
SW_KV_HEADS = 2
SW_GROUP = SW_HEADS // SW_KV_HEADS
SW_WINDOW = 128
DIL_PAIRS = ((128, 1), (512, 4), (2048, 16))
DIL_HPG = 3
DIL_HEADS = DIL_HPG * len(DIL_PAIRS)
MLA_HEADS = 7
MLA_Q_RANK = 896
MLA_KV_RANK = 320
MLA_NOPE = 128
MLA_ROPE = 64
MLA_V = 128

A_W = NA_HEADS * HEAD_DIM
B_QW = SW_HEADS * HEAD_DIM
B_KW = SW_KV_HEADS * HEAD_DIM
C_W = DIL_HEADS * HEAD_DIM
D_OUT_W = MLA_HEADS * MLA_V
MAIN_W = 3 * A_W + B_QW + 2 * B_KW + 3 * C_W

LANE = 128
SUBLANE = 8
MXU_TILE = 1024
VMEM_LIMIT = 56 * 1024 * 1024

QA_BLK, KA_BLK, VA_BLK = 0, A_W // LANE, 2 * A_W // LANE
QB_BLK = 3 * A_W // LANE
KB_BLK = QB_BLK + B_QW // LANE
VB_BLK = KB_BLK + B_KW // LANE
QC_BLK = VB_BLK + B_KW // LANE
KC_BLK = QC_BLK + C_W // LANE
VC_BLK = KC_BLK + C_W // LANE

KV_PAD = 384
MLA_IN_W = MLA_Q_RANK + KV_PAD + LANE
MLA_QK = 2 * LANE

_NT = (((1,), (1,)), ((), ()))
LOG2_E = 1.4426950408889634
LN_2 = 0.6931471805599453


def _round_up(x, m):
    return (x + m - 1) // m * m


def _params(sem, vmem=VMEM_LIMIT):
    return pltpu.CompilerParams(dimension_semantics=sem, vmem_limit_bytes=vmem)


def _rms_kernel(x_ref, g_ref, o_ref):
    x = x_ref[...]
    ms = jnp.mean(x * x, axis=-1, keepdims=True)
    o_ref[...] = ((x * lax.rsqrt(ms + NORM_EPS)) * g_ref[...]).astype(o_ref.dtype)


def rms_rows(x, g, out_dtype):
    t, d = x.shape
    tm = min(256, t)
    return pl.pallas_call(
        _rms_kernel,
        out_shape=jax.ShapeDtypeStruct((t, d), out_dtype),
        grid=(t // tm,),
        in_specs=[pl.BlockSpec((tm, d), lambda i: (i, 0)), pl.BlockSpec((1, d), lambda i: (0, 0))],
        out_specs=pl.BlockSpec((tm, d), lambda i: (i, 0)),
        compiler_params=_params(("parallel",)),
        name="rms_rows",
    )(x, g.reshape(1, d))


def _mm_kernel(a_ref, b_ref, o_ref):
    o_ref[...] = jnp.dot(a_ref[...], b_ref[...], preferred_element_type=F32).astype(o_ref.dtype)


def _mm_res_kernel(a_ref, b_ref, r_ref, o_ref):
    o_ref[...] = r_ref[...] + jnp.dot(a_ref[...], b_ref[...], preferred_element_type=F32)


def _mm_res_acc_kernel(a_ref, b_ref, r_ref, o_ref, acc_ref):
    k = pl.program_id(2)

    @pl.when(k == 0)
    def _():
        acc_ref[...] = jnp.zeros_like(acc_ref)

    acc_ref[...] += jnp.dot(a_ref[...], b_ref[...], preferred_element_type=F32)

    @pl.when(k == pl.num_programs(2) - 1)
    def _():
        o_ref[...] = r_ref[...] + acc_ref[...]


def matmul(a, b, out_dtype, tm=MXU_TILE, tn=MXU_TILE):
    m, k = a.shape
    n = b.shape[1]
    tm, tn = min(tm, m), min(tn, n)
    return pl.pallas_call(
        _mm_kernel,
        out_shape=jax.ShapeDtypeStruct((m, n), out_dtype),
        grid=(m // tm, n // tn),
        in_specs=[pl.BlockSpec((tm, k), lambda i, j: (i, 0)), pl.BlockSpec((k, tn), lambda i, j: (0, j))],
        out_specs=pl.BlockSpec((tm, tn), lambda i, j: (i, j)),
        compiler_params=_params(("parallel", "parallel")),
        name="matmul",
    )(a, b)


def matmul_residual(a, b, res, tm=MXU_TILE, tn=MXU_TILE // 2, tk=None):
    m, k = a.shape
    n = b.shape[1]
    tm, tn = min(tm, m), min(tn, n)
    if tk is None or tk >= k:
        return pl.pallas_call(
            _mm_res_kernel,
            out_shape=jax.ShapeDtypeStruct((m, n), F32),
            grid=(m // tm, n // tn),
            in_specs=[
                pl.BlockSpec((tm, k), lambda i, j: (i, 0)),
                pl.BlockSpec((k, tn), lambda i, j: (0, j)),
                pl.BlockSpec((tm, tn), lambda i, j: (i, j)),
            ],
            out_specs=pl.BlockSpec((tm, tn), lambda i, j: (i, j)),
            compiler_params=_params(("parallel", "parallel")),
            name="matmul_residual",
        )(a, b, res)
    return pl.pallas_call(
        _mm_res_acc_kernel,
        out_shape=jax.ShapeDtypeStruct((m, n), F32),
        grid=(m // tm, n // tn, k // tk),
        in_specs=[
            pl.BlockSpec((tm, tk), lambda i, j, kk: (i, kk)),
            pl.BlockSpec((tk, tn), lambda i, j, kk: (kk, j)),
            pl.BlockSpec((tm, tn), lambda i, j, kk: (i, j)),
        ],
        out_specs=pl.BlockSpec((tm, tn), lambda i, j, kk: (i, j)),
        scratch_shapes=[pltpu.VMEM((tm, tn), F32)],
        compiler_params=_params(("parallel", "parallel", "arbitrary")),
        name="matmul_residual_ksplit",
    )(a, b, res)


def _conv_gate_kernel(g_ref, u_ref, wg_ref, wu_ref, bg_ref, bu_ref, o_ref):
    s_len = g_ref.shape[0]
    row = lax.broadcasted_iota(jnp.int32, g_ref.shape, 0)

    def conv(x_ref, w_ref, b_ref):
        x = x_ref[...].astype(F32)
        prev = jnp.where(row == 0, 0.0, pltpu.roll(x, 1, 0))
        nxt = jnp.where(row == s_len - 1, 0.0, pltpu.roll(x, s_len - 1, 0))
        w = w_ref[...]
        return prev * w[0:1] + x * w[1:2] + nxt * w[2:3] + b_ref[...]

    gate = conv(g_ref, wg_ref, bg_ref)
    up = conv(u_ref, wu_ref, bu_ref)
    o_ref[...] = ((gate / (1.0 + jnp.exp2(gate * (-LOG2_E)))) * up).astype(o_ref.dtype)


def conv_gate(u3, conv_w, conv_b):
    b, s_len, two_f = u3.shape
    fp = two_f // 2
    tc = LANE
    nb = fp // tc
    return pl.pallas_call(
        _conv_gate_kernel,
        out_shape=jax.ShapeDtypeStruct((b, s_len, fp), BF16),
        grid=(b, nb),
        in_specs=[
            pl.BlockSpec((None, s_len, tc), lambda bi, j: (bi, 0, j)),
            pl.BlockSpec((None, s_len, tc), lambda bi, j: (bi, 0, j + nb)),
            pl.BlockSpec((3, tc), lambda bi, j: (0, j)),
            pl.BlockSpec((3, tc), lambda bi, j: (0, j + nb)),
            pl.BlockSpec((1, tc), lambda bi, j: (0, j)),
            pl.BlockSpec((1, tc), lambda bi, j: (0, j + nb)),
        ],
        out_specs=pl.BlockSpec((None, s_len, tc), lambda bi, j: (bi, 0, j)),
        compiler_params=_params(("parallel", "parallel")),
        name="conv_gate",
    )(u3, u3, conv_w, conv_w, conv_b, conv_b)


def _rope(x, cos, sin_signed):
    return x * cos + pltpu.roll(x, LANE // 2, 1) * sin_signed


def rope_tables(s_len, d):
    inv = 1.0 / (ROPE_THETA ** (jnp.arange(0, d, 2, dtype=F32) / d))
    ang = jnp.arange(s_len, dtype=F32)[:, None] * inv[None, :]
    cos, sin = jnp.cos(ang), jnp.sin(ang)
    pad = jnp.zeros((s_len, LANE // 2 - d // 2), F32)
    return (jnp.concatenate([cos, pad, cos, pad], axis=-1), jnp.concatenate([-sin, pad, sin, pad], axis=-1))


NA_GROUP = 4
NA_WIN_ROWS = 12
NA_FIRST, NA_INTERIOR, NA_LAST = 0, 1, 2


def _na_kernel(q_ref, k_ref, v_ref, bias_ref, o_ref, *, rows, unroll):
    scale = HEAD_DIM**-0.5
    n_groups = rows // NA_GROUP
    gq = NA_GROUP * GRID_W
    wk = NA_WIN_ROWS * GRID_W

    def one_group(g):
        r0 = g * NA_GROUP
        ws = jnp.clip(r0 - NA_ROWS // 2, 0, rows - NA_WIN_ROWS)
        variant = jnp.where(g == 0, NA_FIRST, jnp.where(g == n_groups - 1, NA_LAST, NA_INTERIOR))
        q0 = pl.multiple_of(r0 * GRID_W, gq)
        k0 = pl.multiple_of(ws * GRID_W, GRID_W)
        s = lax.dot_general(q_ref[pl.ds(q0, gq), :], k_ref[pl.ds(k0, wk), :], _NT, preferred_element_type=F32)
        s = s * (scale * LOG2_E) + bias_ref[variant]
        m = jnp.max(s, axis=-1, keepdims=True)
        p = jnp.exp2(s - m)
        l = jnp.sum(p, axis=-1, keepdims=True)
        o = jnp.dot(p.astype(BF16), v_ref[pl.ds(k0, wk), :], preferred_element_type=F32) / l
        o_ref[pl.ds(q0, gq), :] = o.astype(o_ref.dtype)

    def body(it, carry):
        for u in range(unroll):
            one_group(it * unroll + u)
        return carry

    lax.fori_loop(0, n_groups // unroll, body, 0)


def na_bias_tables(rpb_all):
    qc = np.arange(GRID_W)[:, None]
    kc = np.arange(GRID_W)[None, :]
    q_ws = np.clip(qc - NA_COLS // 2, 0, GRID_W - NA_COLS)
    valid = (kc >= q_ws) & (kc < q_ws + NA_COLS)
    coff = np.clip(kc - qc + NA_COLS - 1, 0, 2 * NA_COLS - 2)
    onehot = (coff[None] == np.arange(2 * NA_COLS - 1)[:, None, None]).astype(np.float32)
    e = jnp.einsum("lhrc,cqk->lhrqk", rpb_all.astype(F32), onehot, precision=lax.Precision.HIGHEST)
    e = jnp.where(valid, e * LOG2_E, NEG_INF)
    pad = NA_WIN_ROWS
    e_pad = jnp.pad(e, ((0, 0), (0, 0), (pad, pad), (0, 0), (0, 0)))
    i = np.arange(NA_WIN_ROWS)
    slabs, valid_rows = [], []
    for r0_minus_ws, first_valid in ((0, lambda u: 0), (NA_ROWS // 2, lambda u: u), (NA_ROWS, lambda u: NA_ROWS // 2)):
        for u in range(NA_GROUP):
            lo = pad + (NA_ROWS - 1) - u - r0_minus_ws
            slabs.append(e_pad[:, :, lo : lo + NA_WIN_ROWS])
            valid_rows.append((i >= first_valid(u)) & (i < first_valid(u) + NA_ROWS))
    tab = jnp.stack(slabs, axis=2)
    valid_rows = np.stack(valid_rows)[:, :, None, None]
    tab = jnp.where(valid_rows, tab, NEG_INF)
    tab = tab.transpose(0, 1, 2, 4, 3, 5)
    return tab.reshape(tab.shape[:2] + (3, NA_GROUP * GRID_W, NA_WIN_ROWS * GRID_W))


def na_attention(proj3, bias_tab):
    b, s_len, _ = proj3.shape
    rows = s_len // GRID_W
    assert rows >= NA_WIN_ROWS and rows % NA_GROUP == 0, "grid too short for the grouped neighbourhood kernel"
    n_groups = rows // NA_GROUP
    blk = lambda off: pl.BlockSpec((None, s_len, LANE), lambda bi, h, off=off: (bi, 0, off + h))
    return pl.pallas_call(
        functools.partial(_na_kernel, rows=rows, unroll=4 if n_groups % 4 == 0 else 1),
        out_shape=jax.ShapeDtypeStruct((b, s_len, A_W), BF16),
        grid=(b, NA_HEADS),
        in_specs=[
            blk(QA_BLK),
            blk(KA_BLK),
            blk(VA_BLK),
            pl.BlockSpec((None, 3, NA_GROUP * GRID_W, NA_WIN_ROWS * GRID_W), lambda bi, h: (h, 0, 0, 0)),
        ],
        out_specs=pl.BlockSpec((None, s_len, LANE), lambda bi, h: (bi, 0, h)),
        compiler_params=_params(("parallel", "parallel")),
        name="na_attention",
    )(proj3, proj3, proj3, bias_tab)


def _band_kernel(*refs, s_len, dil, tq, hw, win, has_sink, want_lse, unroll):
    refs = list(refs)
    q_ref, k_ref, v_ref, cos_ref, sin_ref = refs[:5]
    del refs[:5]
    sink_ref = refs.pop(0) if has_sink else None
    o_ref = refs.pop(0)
    lse_ref = refs.pop(0) if want_lse else None
    stage, qd, kd, vd = refs[:4]
    od = refs[4] if dil > 1 else None
    ld = refs[5] if (dil > 1 and want_lse) else None
    sub = s_len // dil
    nq = sub // tq
    scale = HEAD_DIM**-0.5

    def regroup(val, dst):
        if dil == 1:
            dst[0] = val.astype(BF16)
        else:
            stage[...] = val
            for r in range(dil):
                dst[r] = stage[pl.ds(r, sub, stride=dil), :].astype(BF16)

    cos, sin = cos_ref[...], sin_ref[...]
    regroup(_rope(k_ref[...].astype(F32), cos, sin), kd)
    regroup(_rope(q_ref[...].astype(F32), cos, sin), qd)
    regroup(v_ref[...].astype(F32), vd)

    def one_block(it):
        r = it // nq
        q0 = pl.multiple_of((it % nq) * tq, tq)
        ws = pl.multiple_of(jnp.clip(q0 - hw, 0, sub - win), 64)
        s = lax.dot_general(qd[r, pl.ds(q0, tq), :], kd[r, pl.ds(ws, win), :], _NT, preferred_element_type=F32)
        s = s * (scale * LOG2_E)
        qpos = q0 + lax.broadcasted_iota(jnp.int32, (tq, win), 0)
        kpos = ws + lax.broadcasted_iota(jnp.int32, (tq, win), 1)
        s = jnp.where(jnp.abs(kpos - qpos) <= hw, s, NEG_INF)
        m = jnp.max(s, axis=-1, keepdims=True)
        if has_sink:
            sk = sink_ref[0:1, 0:1] * LOG2_E
            m = jnp.maximum(m, sk)
        p = jnp.exp2(s - m)
        l = jnp.sum(p, axis=-1, keepdims=True)
        if has_sink:
            l = l + jnp.exp2(sk - m)
        o = jnp.dot(p.astype(BF16), vd[r, pl.ds(ws, win), :], preferred_element_type=F32) / l
        if dil == 1:
            o_ref[pl.ds(q0, tq), :] = o.astype(o_ref.dtype)
        else:
            od[r, pl.ds(q0, tq), :] = o
        if want_lse:
            lse = jnp.broadcast_to(m * LN_2 + jnp.log(l), (tq, LANE))
            if dil == 1:
                lse_ref[pl.ds(q0, tq), :] = lse
            else:
                ld[r, pl.ds(q0, tq), :] = lse

    def body(g, carry):
        for u in range(unroll):
            one_block(g * unroll + u)
        return carry

    lax.fori_loop(0, dil * nq // unroll, body, 0)

    if dil > 1:
        for r in range(dil):
            stage[pl.ds(r, sub, stride=dil), :] = od[r]
        o_ref[...] = stage[...].astype(o_ref.dtype)
        if want_lse:
            for r in range(dil):
                lse_ref[pl.ds(r, sub, stride=dil), :] = ld[r]


def banded_attention(proj3, cos, sin, *, dil, hw, n_heads, q_blk, k_blk, v_blk, kv_group, sink=None, want_lse=False):
    b, s_len, _ = proj3.shape
    sub = s_len // dil
    tq = min(max(128, 2 * hw), sub)
    win = min(tq + 2 * hw, sub)
    total = dil * (sub // tq)
    unroll = 4 if total % 4 == 0 else 1
    out_w = n_heads * LANE

    def col(off, group=1):
        return pl.BlockSpec((None, s_len, LANE), lambda bi, h: (bi, 0, off + h // group))

    table = pl.BlockSpec((s_len, LANE), lambda bi, h: (0, 0))
    in_specs = [col(q_blk), col(k_blk, kv_group), col(v_blk, kv_group), table, table]
    args = [proj3, proj3, proj3, cos, sin]
    if sink is not None:
        in_specs.append(pl.BlockSpec((None, SUBLANE, LANE), lambda bi, h: (h, 0, 0)))
        args.append(jnp.broadcast_to(sink.astype(F32)[:, None, None], (n_heads, SUBLANE, LANE)))
    out_spec = pl.BlockSpec((None, s_len, LANE), lambda bi, h: (bi, 0, h))
    out_shape = [jax.ShapeDtypeStruct((b, s_len, out_w), BF16)]
    out_specs = [out_spec]
    if want_lse:
        out_shape.append(jax.ShapeDtypeStruct((b, s_len, out_w), F32))
        out_specs.append(out_spec)
    slab = (dil, sub, LANE)
    scratch = [pltpu.VMEM((s_len, LANE), F32), pltpu.VMEM(slab, BF16), pltpu.VMEM(slab, BF16), pltpu.VMEM(slab, BF16)]
    if dil > 1:
        scratch.append(pltpu.VMEM(slab, F32))
        if want_lse:
            scratch.append(pltpu.VMEM(slab, F32))
    return pl.pallas_call(
        functools.partial(
            _band_kernel, s_len=s_len, dil=dil, tq=tq, hw=hw, win=win, has_sink=sink is not None,
            want_lse=want_lse, unroll=unroll,
        ),
        out_shape=out_shape,
        grid=(b, n_heads),
        in_specs=in_specs,
        out_specs=out_specs,
        scratch_shapes=scratch,
        compiler_params=_params(("parallel", "arbitrary")),
        name=f"banded_attention_d{dil}",
    )(*args)


def _merge_kernel(o0_ref, o1_ref, o2_ref, l0_ref, l1_ref, l2_ref, out_ref):
    l0, l1, l2 = l0_ref[...], l1_ref[...], l2_ref[...]
    mx = jnp.maximum(jnp.maximum(l0, l1), l2)
    e0, e1, e2 = jnp.exp(l0 - mx), jnp.exp(l1 - mx), jnp.exp(l2 - mx)
    den = e0 + e1 + e2
    gw = DIL_HPG * LANE
    for g, (o_ref, e) in enumerate(((o0_ref, e0), (o1_ref, e1), (o2_ref, e2))):
        out_ref[:, g * gw : (g + 1) * gw] = (o_ref[...].astype(F32) * (e / den)).astype(out_ref.dtype)


def merge_dilated(outs, lses):
    t, gw = outs[0].shape
    tm = min(512, t)
    spec = pl.BlockSpec((tm, gw), lambda i: (i, 0))
    return pl.pallas_call(
        _merge_kernel,
        out_shape=jax.ShapeDtypeStruct((t, len(outs) * gw), BF16),
        grid=(t // tm,),
        in_specs=[spec] * 6,
        out_specs=pl.BlockSpec((tm, len(outs) * gw), lambda i: (i, 0)),
        compiler_params=_params(("parallel",)),
        name="merge_dilated",
    )(*outs, *lses)


def _mla_pre_kernel(h_ref, w1_ref, gq_ref, gkv_ref, wuq_ref, wukv_ref, cos_ref, sin_ref, q_ref, k_ref, v_ref):
    c = jnp.dot(h_ref[...], w1_ref[...], preferred_element_type=F32)
    cq = c[:, :MLA_Q_RANK]
    ckv = c[:, MLA_Q_RANK : MLA_Q_RANK + KV_PAD]
    kr = c[:, MLA_Q_RANK + KV_PAD :]
    cos, sin = cos_ref[...], sin_ref[...]

    nq = (cq * lax.rsqrt(jnp.mean(cq * cq, axis=-1, keepdims=True) + NORM_EPS)) * gq_ref[...]
    q = jnp.dot(nq.astype(BF16), wuq_ref[...], preferred_element_type=F32)
    ms = jnp.sum(ckv * ckv, axis=-1, keepdims=True) * (1.0 / MLA_KV_RANK)
    nkv = (ckv * lax.rsqrt(ms + NORM_EPS)) * gkv_ref[...]
    kv = jnp.dot(nkv.astype(BF16), wukv_ref[...], preferred_element_type=F32)
    k_rope = _rope(kr, cos, sin).astype(k_ref.dtype)
    for h in range(MLA_HEADS):
        lo = h * MLA_QK
        q_ref[:, lo : lo + LANE] = q[:, lo : lo + LANE].astype(q_ref.dtype)
        q_ref[:, lo + LANE : lo + MLA_QK] = _rope(q[:, lo + LANE : lo + MLA_QK], cos, sin).astype(q_ref.dtype)
        k_ref[:, lo : lo + LANE] = kv[:, h * LANE : (h + 1) * LANE].astype(k_ref.dtype)
        k_ref[:, lo + LANE : lo + MLA_QK] = k_rope
    v_ref[...] = kv[:, D_OUT_W:].astype(v_ref.dtype)


def mla_pre(h, w1, gq, gkv, wuq, wukv, cos, sin, s_len):
    t, d = h.shape
    tm = min(256, s_len)
    n_pos = s_len // tm
    const = lambda shape: pl.BlockSpec(shape, lambda i: (0, 0))
    row = lambda w: pl.BlockSpec((tm, w), lambda i: (i, 0))
    return pl.pallas_call(
        _mla_pre_kernel,
        out_shape=[
            jax.ShapeDtypeStruct((t, MLA_HEADS * MLA_QK), BF16),
            jax.ShapeDtypeStruct((t, MLA_HEADS * MLA_QK), BF16),
            jax.ShapeDtypeStruct((t, D_OUT_W), BF16),
        ],
        grid=(t // tm,),
        in_specs=[
            row(d),
            const((d, MLA_IN_W)),
            const((1, MLA_Q_RANK)),
            const((1, KV_PAD)),
            const((MLA_Q_RANK, MLA_HEADS * MLA_QK)),
            const((KV_PAD, 2 * D_OUT_W)),
            pl.BlockSpec((tm, LANE), lambda i: (i % n_pos, 0)),
            pl.BlockSpec((tm, LANE), lambda i: (i % n_pos, 0)),
        ],
        out_specs=[row(MLA_HEADS * MLA_QK), row(MLA_HEADS * MLA_QK), row(D_OUT_W)],
        compiler_params=_params(("parallel",)),
        name="mla_pre",
    )(h, w1, gq, gkv, wuq, wukv, cos, sin)


def _mla_kernel(q_ref, k_ref, v_ref, o_ref, s_a, s_b, *, tq):
    n_blk = q_ref.shape[0] // tq
    c = (MLA_NOPE + MLA_ROPE) ** -0.5 * LOG2_E

    def scores(i, dst):
        q0 = pl.multiple_of(jnp.minimum(i, n_blk - 1) * tq, tq)
        dst[...] = lax.dot_general(q_ref[pl.ds(q0, tq), :], k_ref[...], _NT, preferred_element_type=F32)

    def finish(src, i):
        s = src[...] * c
        m = jnp.max(s, axis=-1, keepdims=True)
        p = jnp.exp2(s - m)
        l = jnp.sum(p, axis=-1, keepdims=True)
        o = jnp.dot(p.astype(BF16), v_ref[...], preferred_element_type=F32) / l
        o_ref[pl.ds(pl.multiple_of(i * tq, tq), tq), :] = o.astype(o_ref.dtype)

    scores(0, s_a)

    def pair(g, carry):
        i = 2 * g
        scores(i + 1, s_b)
        finish(s_a, i)
        scores(i + 2, s_a)
        finish(s_b, i + 1)
        return carry

    lax.fori_loop(0, n_blk // 2, pair, 0)


def mla_attention(q, k, v):
    b, s_len, _ = q.shape
    tq = min(512, s_len // 2)
    blk = lambda w: pl.BlockSpec((None, s_len, w), lambda bi, h: (bi, 0, h))
    return pl.pallas_call(
        functools.partial(_mla_kernel, tq=tq),
        out_shape=jax.ShapeDtypeStruct((b, s_len, D_OUT_W), BF16),
        grid=(b, MLA_HEADS),
        in_specs=[blk(MLA_QK), blk(MLA_QK), blk(LANE)],
        out_specs=blk(LANE),
        scratch_shapes=[pltpu.VMEM((tq, s_len), F32), pltpu.VMEM((tq, s_len), F32)],
        compiler_params=_params(("parallel", "parallel")),
        name="mla_attention",
    )(q, k, v)


def _pad_cols(w, width):
    return jnp.pad(w, ((0, 0),) * (w.ndim - 1) + ((0, width - w.shape[-1]),))


def _interleave_rope(w):
    z = jnp.zeros(w.shape[:-1] + (LANE // 2 - MLA_ROPE // 2,), w.dtype)
    return jnp.concatenate([w[..., : MLA_ROPE // 2], z, w[..., MLA_ROPE // 2 :], z], axis=-1)


def _gate_up_blocks(w, f, fp):
    return jnp.concatenate([_pad_cols(w[..., :f], fp), _pad_cols(w[..., f:], fp)], axis=-1)


def prepare_layer(w_in, w_uq, mla_kv_norm, w_ukv, w_out, w_up, conv_w, conv_b, w_down):
    f = w_down.shape[0]
    fp = _round_up(f, MXU_TILE)
    main_w = _round_up(MAIN_W, MXU_TILE)
    w_main = _pad_cols(w_in[:, :MAIN_W], main_w).astype(BF16)
    o = MAIN_W
    w_mla = jnp.concatenate(
        [
            w_in[:, o : o + MLA_Q_RANK],
            _pad_cols(w_in[:, o + MLA_Q_RANK : o + MLA_Q_RANK + MLA_KV_RANK], KV_PAD),
            _interleave_rope(w_in[:, o + MLA_Q_RANK + MLA_KV_RANK :]),
        ],
        axis=-1,
    ).astype(BF16)
    uq = w_uq.reshape(MLA_Q_RANK, MLA_HEADS, MLA_NOPE + MLA_ROPE)
    wuq = jnp.concatenate([uq[..., :MLA_NOPE], _interleave_rope(uq[..., MLA_NOPE:])], axis=-1)
    wuq = wuq.reshape(MLA_Q_RANK, MLA_HEADS * MLA_QK).astype(BF16)
    ukv = w_ukv.reshape(MLA_KV_RANK, MLA_HEADS, MLA_NOPE + MLA_V)
    wukv = jnp.concatenate(
        [ukv[..., :MLA_NOPE].reshape(MLA_KV_RANK, D_OUT_W), ukv[..., MLA_NOPE:].reshape(MLA_KV_RANK, D_OUT_W)], axis=-1
    )
    wukv = jnp.pad(wukv, ((0, KV_PAD - MLA_KV_RANK), (0, 0))).astype(BF16)
    gkv = _pad_cols(mla_kv_norm.reshape(1, MLA_KV_RANK), KV_PAD)
    return dict(
        w_main=w_main,
        w_mla=w_mla,
        wuq=wuq,
        wukv=wukv,
        gkv=gkv,
        w_out=w_out.astype(BF16),
        w_up=_gate_up_blocks(w_up, f, fp).astype(BF16),
        conv_w=_gate_up_blocks(conv_w, f, fp),
        conv_b=_gate_up_blocks(conv_b[None], f, fp),
        w_down=jnp.pad(w_down, ((0, fp - f), (0, 0))).astype(BF16),
    )


def encoder_layer(x, b, s_len, lw, na_bias, attn_norm, sink, mla_q_norm, ffn_norm, tabs):
    t, d = x.shape
    cos128, sin128, cos64, sin64 = tabs
    h = rms_rows(x, attn_norm, BF16)
    proj = matmul(h, lw["w_main"], BF16)
    proj3 = proj.reshape(b, s_len, proj.shape[1])

    oa = na_attention(proj3, na_bias)

    (ob,) = banded_attention(
        proj3, cos128, sin128, dil=1, hw=SW_WINDOW, n_heads=SW_HEADS, q_blk=QB_BLK, k_blk=KB_BLK, v_blk=VB_BLK,
        kv_group=SW_GROUP, sink=sink,
    )

    outs, lses = [], []
    for gi, (win, dil) in enumerate(DIL_PAIRS):
        o_g, lse_g = banded_attention(
            proj3, cos128, sin128, dil=dil, hw=(win // 2) // dil, n_heads=DIL_HPG, q_blk=QC_BLK + gi * DIL_HPG,
            k_blk=KC_BLK + gi * DIL_HPG, v_blk=VC_BLK + gi * DIL_HPG, kv_group=1, want_lse=True,
        )
        outs.append(o_g.reshape(t, DIL_HPG * LANE))
        lses.append(lse_g.reshape(t, DIL_HPG * LANE))
    oc = merge_dilated(outs, lses)

    q, k, v = mla_pre(
        h, lw["w_mla"], mla_q_norm.reshape(1, MLA_Q_RANK), lw["gkv"], lw["wuq"], lw["wukv"], cos64, sin64, s_len
    )
    od = mla_attention(q.reshape(b, s_len, -1), k.reshape(b, s_len, -1), v.reshape(b, s_len, -1))

    mix = jnp.concatenate([oa.reshape(t, A_W), ob.reshape(t, B_QW), oc, od.reshape(t, D_OUT_W)], axis=-1)
    x = matmul_residual(mix, lw["w_out"], x)

    h2 = rms_rows(x, ffn_norm, BF16)
    u = matmul(h2, lw["w_up"], BF16)
    act = conv_gate(u.reshape(b, s_len, u.shape[1]), lw["conv_w"], lw["conv_b"])
    fp = act.shape[-1]
    return matmul_residual(act.reshape(t, fp), lw["w_down"], x, tm=MXU_TILE // 2, tn=MXU_TILE // 2)


def trunk(x3, layers, na_bias, attn_norm, sink, mla_q_norm, ffn_norm, final_norm):
    b, s_len, d = x3.shape
    tabs = rope_tables(s_len, HEAD_DIM) + rope_tables(s_len, MLA_ROPE)
    x = x3.reshape(b * s_len, d)
    for l, lw in enumerate(layers):
        x = encoder_layer(x, b, s_len, lw, na_bias[l], attn_norm[l], sink[l], mla_q_norm[l], ffn_norm[l], tabs)
    return rms_rows(x, final_norm, F32).reshape(b, s_len, d)


def kernel(x_prompt, x_sample, attn_norm, w_in, na_rpb, sink, mla_q_norm, w_uq, mla_kv_norm, w_ukv, w_out, ffn_norm, w_up, conv_w, conv_b, w_down, final_norm):
    depth = w_in.shape[0]
    layers = [
        prepare_layer(w_in[l], w_uq[l], mla_kv_norm[l], w_ukv[l], w_out[l], w_up[l], conv_w[l], conv_b[l], w_down[l])
        for l in range(depth)
    ]
    na_bias = na_bias_tables(na_rpb)
    return tuple(
        trunk(x3, layers, na_bias, attn_norm, sink, mla_q_norm, ffn_norm, final_norm) for x3 in (x_prompt, x_sample)
    )
```

```python
import functools

import numpy as np
import jax
import jax.numpy as jnp
from jax import lax
from jax.experimental import pallas as pl
from jax.experimental.pallas import tpu as pltpu

F32 = jnp.float32
BF16 = jnp.bfloat16

HEAD_DIM = 128
ROPE_THETA = 10000.0
NORM_EPS = 1e-6
NEG_INF = -1e30
GRID_W = 64
NA_HEADS = 8
NA_ROWS = 8
NA_COLS = 16
SW_HEADS = 8
SW_KV_HEADS = 2
SW_GROUP = SW_HEADS // SW_KV_HEADS
SW_WINDOW = 128
DIL_PAIRS = ((128, 1), (512, 4), (2048, 16))
DIL_HPG = 3
DIL_HEADS = DIL_HPG * len(DIL_PAIRS)
MLA_HEADS = 7
MLA_Q_RANK = 896
MLA_KV_RANK = 320
MLA_NOPE = 128
MLA_ROPE = 64
MLA_V = 128

A_W = NA_HEADS * HEAD_DIM
B_QW = SW_HEADS * HEAD_DIM
B_KW = SW_KV_HEADS * HEAD_DIM
C_W = DIL_HEADS * HEAD_DIM
D_OUT_W = MLA_HEADS * MLA_V
MAIN_W = 3 * A_W + B_QW + 2 * B_KW + 3 * C_W

LANE = 128
SUBLANE = 8
MXU_TILE = 1024
VMEM_LIMIT = 56 * 1024 * 1024

QA_BLK, KA_BLK, VA_BLK = 0, A_W // LANE, 2 * A_W // LANE
QB_BLK = 3 * A_W // LANE
KB_BLK = QB_BLK + B_QW // LANE
VB_BLK = KB_BLK + B_KW // LANE
QC_BLK = VB_BLK + B_KW // LANE
KC_BLK = QC_BLK + C_W // LANE
VC_BLK = KC_BLK + C_W // LANE

KV_PAD = 384
MLA_IN_W = MLA_Q_RANK + KV_PAD + LANE
MLA_QK = 2 * LANE

_NT = (((1,), (1,)), ((), ()))
LOG2_E = 1.4426950408889634
LN_2 = 0.6931471805599453


def _round_up(x, m):
    return (x + m - 1) // m * m


def _params(sem, vmem=VMEM_LIMIT):
    return pltpu.CompilerParams(dimension_semantics=sem, vmem_limit_bytes=vmem)


def _rms_kernel(x_ref, g_ref, o_ref):
    x = x_ref[...]
    ms = jnp.mean(x * x, axis=-1, keepdims=True)
    o_ref[...] = ((x * lax.rsqrt(ms + NORM_EPS)) * g_ref[...]).astype(o_ref.dtype)


def rms_rows(x, g, out_dtype):
    t, d = x.shape
    tm = min(256, t)
    return pl.pallas_call(
        _rms_kernel,
        out_shape=jax.ShapeDtypeStruct((t, d), out_dtype),
        grid=(t // tm,),
        in_specs=[pl.BlockSpec((tm, d), lambda i: (i, 0)), pl.BlockSpec((1, d), lambda i: (0, 0))],
        out_specs=pl.BlockSpec((tm, d), lambda i: (i, 0)),
        compiler_params=_params(("parallel",)),
        name="rms_rows",
    )(x, g.reshape(1, d))


def _row_scale(ssq_ref, d):
    return lax.rsqrt(ssq_ref[:, 0:1] * (1.0 / d) + NORM_EPS)


def _prenorm_kernel(x_ref, g_ref, h_ref, ssq_ref):
    x = x_ref[...]
    h_ref[...] = (x * g_ref[...]).astype(h_ref.dtype)
    ssq_ref[...] = jnp.broadcast_to(jnp.sum(x * x, axis=-1, keepdims=True), ssq_ref.shape)


def prenorm_rows(x, g):
    t, d = x.shape
    tm = min(256, t)
    return pl.pallas_call(
        _prenorm_kernel,
        out_shape=[jax.ShapeDtypeStruct((t, d), BF16), jax.ShapeDtypeStruct((t, LANE), F32)],
        grid=(t // tm,),
        in_specs=[pl.BlockSpec((tm, d), lambda i: (i, 0)), pl.BlockSpec((1, d), lambda i: (0, 0))],
        out_specs=[pl.BlockSpec((tm, d), lambda i: (i, 0)), pl.BlockSpec((tm, LANE), lambda i: (i, 0))],
        compiler_params=_params(("parallel",)),
        name="prenorm_rows",
    )(x, g.reshape(1, d))


def _largest_tile(n, candidates):
    return next(c for c in candidates if n % c == 0)


def _mm_scaled_kernel(a_ref, b_ref, ssq_ref, o_ref, *, d):
    acc = jnp.dot(a_ref[...], b_ref[...], preferred_element_type=F32)
    o_ref[...] = (acc * _row_scale(ssq_ref, d)).astype(o_ref.dtype)


def matmul_scaled(a, b, ssq, out_dtype, tm, tn):
    m, k = a.shape
    n = b.shape[1]
    tm = min(tm, m)
    tn = _largest_tile(n, (tn, 512, 256, LANE))
    return pl.pallas_call(
        functools.partial(_mm_scaled_kernel, d=k),
        out_shape=jax.ShapeDtypeStruct((m, n), out_dtype),
        grid=(m // tm, n // tn),
        in_specs=[
            pl.BlockSpec((tm, k), lambda i, j: (i, 0)),
            pl.BlockSpec((k, tn), lambda i, j: (0, j)),
            pl.BlockSpec((tm, LANE), lambda i, j: (i, 0)),
        ],
        out_specs=pl.BlockSpec((tm, tn), lambda i, j: (i, j)),
        compiler_params=_params(("parallel", "parallel")),
        name="matmul_scaled",
    )(a, b, ssq)


def _mm_res_kernel(a_ref, b_ref, r_ref, o_ref):
    o_ref[...] = r_ref[...] + jnp.dot(a_ref[...], b_ref[...], preferred_element_type=F32)


def _mm_res_prenorm_kernel(a_ref, b_ref, r_ref, g_ref, o_ref, h_ref, ssq_ref):
    x = r_ref[...] + jnp.dot(a_ref[...], b_ref[...], preferred_element_type=F32)
    o_ref[...] = x
    h_ref[...] = (x * g_ref[...]).astype(h_ref.dtype)

    @pl.when(pl.program_id(1) == 0)
    def _():
        ssq_ref[...] = jnp.zeros_like(ssq_ref)

    ssq_ref[...] += jnp.broadcast_to(jnp.sum(x * x, axis=-1, keepdims=True), ssq_ref.shape)


def matmul_residual(a, b, res, g_next=None, tm=MXU_TILE, tn=MXU_TILE // 2):
    m, k = a.shape
    n = b.shape[1]
    tm, tn = min(tm, m), min(tn, n)
    in_specs = [
        pl.BlockSpec((tm, k), lambda i, j: (i, 0)),
        pl.BlockSpec((k, tn), lambda i, j: (0, j)),
        pl.BlockSpec((tm, tn), lambda i, j: (i, j)),
    ]
    tile = pl.BlockSpec((tm, tn), lambda i, j: (i, j))
    if g_next is None:
        return pl.pallas_call(
            _mm_res_kernel,
            out_shape=jax.ShapeDtypeStruct((m, n), F32),
            grid=(m // tm, n // tn),
            in_specs=in_specs,
            out_specs=tile,
            compiler_params=_params(("parallel", "parallel")),
            name="matmul_residual",
        )(a, b, res)
    return pl.pallas_call(
        _mm_res_prenorm_kernel,
        out_shape=[
            jax.ShapeDtypeStruct((m, n), F32),
            jax.ShapeDtypeStruct((m, n), BF16),
            jax.ShapeDtypeStruct((m, LANE), F32),
        ],
        grid=(m // tm, n // tn),
        in_specs=in_specs + [pl.BlockSpec((1, tn), lambda i, j: (0, j))],
        out_specs=[tile, tile, pl.BlockSpec((tm, LANE), lambda i, j: (i, 0))],
        compiler_params=_params(("parallel", "arbitrary")),
        name="matmul_residual_prenorm",
    )(a, b, res, g_next.reshape(1, n))


def _conv_gate_kernel(g_ref, u_ref, wg_ref, wu_ref, bg_ref, bu_ref, o_ref):
    s_len = g_ref.shape[0]
    row = lax.broadcasted_iota(jnp.int32, g_ref.shape, 0)

    def conv(x_ref, w_ref, b_ref):
        x = x_ref[...].astype(F32)
        prev = jnp.where(row == 0, 0.0, pltpu.roll(x, 1, 0))
        nxt = jnp.where(row == s_len - 1, 0.0, pltpu.roll(x, s_len - 1, 0))
        w = w_ref[...]
        return prev * w[0:1] + x * w[1:2] + nxt * w[2:3] + b_ref[...]

    gate = conv(g_ref, wg_ref, bg_ref)
    up = conv(u_ref, wu_ref, bu_ref)
    o_ref[...] = ((gate / (1.0 + jnp.exp2(gate * (-LOG2_E)))) * up).astype(o_ref.dtype)


def conv_gate(u3, conv_w, conv_b):
    b, s_len, two_f = u3.shape
    fp = two_f // 2
    tc = LANE
    nb = fp // tc
    return pl.pallas_call(
        _conv_gate_kernel,
        out_shape=jax.ShapeDtypeStruct((b, s_len, fp), BF16),
        grid=(b, nb),
        in_specs=[
            pl.BlockSpec((None, s_len, tc), lambda bi, j: (bi, 0, j)),
            pl.BlockSpec((None, s_len, tc), lambda bi, j: (bi, 0, j + nb)),
            pl.BlockSpec((3, tc), lambda bi, j: (0, j)),
            pl.BlockSpec((3, tc), lambda bi, j: (0, j + nb)),
            pl.BlockSpec((1, tc), lambda bi, j: (0, j)),
            pl.BlockSpec((1, tc), lambda bi, j: (0, j + nb)),
        ],
        out_specs=pl.BlockSpec((None, s_len, tc), lambda bi, j: (bi, 0, j)),
        compiler_params=_params(("parallel", "parallel")),
        name="conv_gate",
    )(u3, u3, conv_w, conv_w, conv_b, conv_b)


def _rope(x, cos, sin_signed):
    return x * cos + pltpu.roll(x, LANE // 2, 1) * sin_signed


def rope_tables(s_len, d):
    inv = 1.0 / (ROPE_THETA ** (jnp.arange(0, d, 2, dtype=F32) / d))
    ang = jnp.arange(s_len, dtype=F32)[:, None] * inv[None, :]
    cos, sin = jnp.cos(ang), jnp.sin(ang)
    pad = jnp.zeros((s_len, LANE // 2 - d // 2), F32)
    return (jnp.concatenate([cos, pad, cos, pad], axis=-1), jnp.concatenate([-sin, pad, sin, pad], axis=-1))


NA_GROUP = 4
NA_WIN_ROWS = 12
NA_FIRST, NA_INTERIOR, NA_LAST = 0, 1, 2


def _na_kernel(q_ref, k_ref, v_ref, bias_ref, o_ref, *, rows, unroll):
    scale = HEAD_DIM**-0.5
    n_groups = rows // NA_GROUP
    gq = NA_GROUP * GRID_W
    wk = NA_WIN_ROWS * GRID_W

    def one_group(g):
        r0 = g * NA_GROUP
        ws = jnp.clip(r0 - NA_ROWS // 2, 0, rows - NA_WIN_ROWS)
        variant = jnp.where(g == 0, NA_FIRST, jnp.where(g == n_groups - 1, NA_LAST, NA_INTERIOR))
        q0 = pl.multiple_of(r0 * GRID_W, gq)
        k0 = pl.multiple_of(ws * GRID_W, GRID_W)
        s = lax.dot_general(q_ref[pl.ds(q0, gq), :], k_ref[pl.ds(k0, wk), :], _NT, preferred_element_type=F32)
        s = s * (scale * LOG2_E) + bias_ref[variant]
        m = jnp.max(s, axis=-1, keepdims=True)
        p = jnp.exp2(s - m)
        l = jnp.sum(p, axis=-1, keepdims=True)
        o = jnp.dot(p.astype(BF16), v_ref[pl.ds(k0, wk), :], preferred_element_type=F32) / l
        o_ref[pl.ds(q0, gq), :] = o.astype(o_ref.dtype)

    def body(it, carry):
        for u in range(unroll):
            one_group(it * unroll + u)
        return carry

    lax.fori_loop(0, n_groups // unroll, body, 0)


def na_bias_tables(rpb_all):
    qc = np.arange(GRID_W)[:, None]
    kc = np.arange(GRID_W)[None, :]
    q_ws = np.clip(qc - NA_COLS // 2, 0, GRID_W - NA_COLS)
    valid = (kc >= q_ws) & (kc < q_ws + NA_COLS)
    coff = np.clip(kc - qc + NA_COLS - 1, 0, 2 * NA_COLS - 2)
    onehot = (coff[None] == np.arange(2 * NA_COLS - 1)[:, None, None]).astype(np.float32)
    e = jnp.einsum("lhrc,cqk->lhrqk", rpb_all.astype(F32), onehot, precision=lax.Precision.HIGHEST)
    e = jnp.where(valid, e * LOG2_E, NEG_INF)
    pad = NA_WIN_ROWS
    e_pad = jnp.pad(e, ((0, 0), (0, 0), (pad, pad), (0, 0), (0, 0)))
    i = np.arange(NA_WIN_ROWS)
    slabs, valid_rows = [], []
    for r0_minus_ws, first_valid in ((0, lambda u: 0), (NA_ROWS // 2, lambda u: u), (NA_ROWS, lambda u: NA_ROWS // 2)):
        for u in range(NA_GROUP):
            lo = pad + (NA_ROWS - 1) - u - r0_minus_ws
            slabs.append(e_pad[:, :, lo : lo + NA_WIN_ROWS])
            valid_rows.append((i >= first_valid(u)) & (i < first_valid(u) + NA_ROWS))
    tab = jnp.stack(slabs, axis=2)
    valid_rows = np.stack(valid_rows)[:, :, None, None]
    tab = jnp.where(valid_rows, tab, NEG_INF)
    tab = tab.transpose(0, 1, 2, 4, 3, 5)
    return tab.reshape(tab.shape[:2] + (3, NA_GROUP * GRID_W, NA_WIN_ROWS * GRID_W))


def na_attention(proj3, bias_tab):
    b, s_len, _ = proj3.shape
    rows = s_len // GRID_W
    assert rows >= NA_WIN_ROWS and rows % NA_GROUP == 0, "grid too short for the grouped neighbourhood kernel"
    n_groups = rows // NA_GROUP
    blk = lambda off: pl.BlockSpec((None, s_len, LANE), lambda bi, h, off=off: (bi, 0, off + h))
    return pl.pallas_call(
        functools.partial(_na_kernel, rows=rows, unroll=4 if n_groups % 4 == 0 else 1),
        out_shape=jax.ShapeDtypeStruct((b, s_len, A_W), BF16),
        grid=(b, NA_HEADS),
        in_specs=[
            blk(QA_BLK),
            blk(KA_BLK),
            blk(VA_BLK),
            pl.BlockSpec((None, 3, NA_GROUP * GRID_W, NA_WIN_ROWS * GRID_W), lambda bi, h: (h, 0, 0, 0)),
        ],
        out_specs=pl.BlockSpec((None, s_len, LANE), lambda bi, h: (bi, 0, h)),
        compiler_params=_params(("parallel", "parallel")),
        name="na_attention",
    )(proj3, proj3, proj3, bias_tab)


def _band_kernel(*refs, s_len, dil, tq, hw, win, has_sink, want_lse, unroll):
    refs = list(refs)
    q_ref, k_ref, v_ref, cos_ref, sin_ref = refs[:5]
    del refs[:5]
    sink_ref = refs.pop(0) if has_sink else None
    o_ref = refs.pop(0)
    lse_ref = refs.pop(0) if want_lse else None
    stage, qd, kd, vd = refs[:4]
    od = refs[4] if dil > 1 else None
    ld = refs[5] if (dil > 1 and want_lse) else None
    sub = s_len // dil
    nq = sub // tq
    scale = HEAD_DIM**-0.5

    def regroup(val, dst):
        if dil == 1:
            dst[0] = val.astype(BF16)
        else:
            stage[...] = val
            for r in range(dil):
                dst[r] = stage[pl.ds(r, sub, stride=dil), :].astype(BF16)

    cos, sin = cos_ref[...], sin_ref[...]
    regroup(_rope(k_ref[...].astype(F32), cos, sin), kd)
    regroup(_rope(q_ref[...].astype(F32), cos, sin), qd)
    regroup(v_ref[...].astype(F32), vd)

    def one_block(it):
        r = it // nq
        q0 = pl.multiple_of((it % nq) * tq, tq)
        ws = pl.multiple_of(jnp.clip(q0 - hw, 0, sub - win), 64)
        s = lax.dot_general(qd[r, pl.ds(q0, tq), :], kd[r, pl.ds(ws, win), :], _NT, preferred_element_type=F32)
        s = s * (scale * LOG2_E)
        qpos = q0 + lax.broadcasted_iota(jnp.int32, (tq, win), 0)
        kpos = ws + lax.broadcasted_iota(jnp.int32, (tq, win), 1)
        s = jnp.where(jnp.abs(kpos - qpos) <= hw, s, NEG_INF)
        m = jnp.max(s, axis=-1, keepdims=True)
        if has_sink:
            sk = sink_ref[0:1, 0:1] * LOG2_E
            m = jnp.maximum(m, sk)
        p = jnp.exp2(s - m)
        l = jnp.sum(p, axis=-1, keepdims=True)
        if has_sink:
            l = l + jnp.exp2(sk - m)
        o = jnp.dot(p.astype(BF16), vd[r, pl.ds(ws, win), :], preferred_element_type=F32) / l
        if dil == 1:
            o_ref[pl.ds(q0, tq), :] = o.astype(o_ref.dtype)
        else:
            od[r, pl.ds(q0, tq), :] = o
        if want_lse:
            lse = jnp.broadcast_to(m * LN_2 + jnp.log(l), (tq, LANE))
            if dil == 1:
                lse_ref[pl.ds(q0, tq), :] = lse
            else:
                ld[r, pl.ds(q0, tq), :] = lse

    def body(g, carry):
        for u in range(unroll):
            one_block(g * unroll + u)
        return carry

    lax.fori_loop(0, dil * nq // unroll, body, 0)

    if dil > 1:
        for r in range(dil):
            stage[pl.ds(r, sub, stride=dil), :] = od[r]
        o_ref[...] = stage[...].astype(o_ref.dtype)
        if want_lse:
            for r in range(dil):
                lse_ref[pl.ds(r, sub, stride=dil), :] = ld[r]


def banded_attention(proj3, cos, sin, *, dil, hw, n_heads, q_blk, k_blk, v_blk, kv_group, sink=None, want_lse=False):
    b, s_len, _ = proj3.shape
    sub = s_len // dil
    tq = min(max(128, 2 * hw), sub)
    win = min(tq + 2 * hw, sub)
    total = dil * (sub // tq)
    unroll = 4 if total % 4 == 0 else 1
    out_w = n_heads * LANE

    def col(off, group=1):
        return pl.BlockSpec((None, s_len, LANE), lambda bi, h: (bi, 0, off + h // group))

    table = pl.BlockSpec((s_len, LANE), lambda bi, h: (0, 0))
    in_specs = [col(q_blk), col(k_blk, kv_group), col(v_blk, kv_group), table, table]
    args = [proj3, proj3, proj3, cos, sin]
    if sink is not None:
        in_specs.append(pl.BlockSpec((None, SUBLANE, LANE), lambda bi, h: (h, 0, 0)))
        args.append(jnp.broadcast_to(sink.astype(F32)[:, None, None], (n_heads, SUBLANE, LANE)))
    out_spec = pl.BlockSpec((None, s_len, LANE), lambda bi, h: (bi, 0, h))
    out_shape = [jax.ShapeDtypeStruct((b, s_len, out_w), BF16)]
    out_specs = [out_spec]
    if want_lse:
        out_shape.append(jax.ShapeDtypeStruct((b, s_len, out_w), F32))
        out_specs.append(out_spec)
    slab = (dil, sub, LANE)
    scratch = [pltpu.VMEM((s_len, LANE), F32), pltpu.VMEM(slab, BF16), pltpu.VMEM(slab, BF16), pltpu.VMEM(slab, BF16)]
    if dil > 1:
        scratch.append(pltpu.VMEM(slab, F32))
        if want_lse:
            scratch.append(pltpu.VMEM(slab, F32))
    return pl.pallas_call(
        functools.partial(
            _band_kernel, s_len=s_len, dil=dil, tq=tq, hw=hw, win=win, has_sink=sink is not None,
            want_lse=want_lse, unroll=unroll,
        ),
        out_shape=out_shape,
        grid=(b, n_heads),
        in_specs=in_specs,
        out_specs=out_specs,
        scratch_shapes=scratch,
        compiler_params=_params(("parallel", "arbitrary")),
        name=f"banded_attention_d{dil}",
    )(*args)


def _merge_kernel(o0_ref, o1_ref, o2_ref, l0_ref, l1_ref, l2_ref, out_ref):
    l0, l1, l2 = l0_ref[...], l1_ref[...], l2_ref[...]
    mx = jnp.maximum(jnp.maximum(l0, l1), l2)
    e0, e1, e2 = jnp.exp(l0 - mx), jnp.exp(l1 - mx), jnp.exp(l2 - mx)
    den = e0 + e1 + e2
    gw = DIL_HPG * LANE
    for g, (o_ref, e) in enumerate(((o0_ref, e0), (o1_ref, e1), (o2_ref, e2))):
        out_ref[:, g * gw : (g + 1) * gw] = (o_ref[...].astype(F32) * (e / den)).astype(out_ref.dtype)


def merge_dilated(outs, lses):
    t, gw = outs[0].shape
    tm = min(512, t)
    spec = pl.BlockSpec((tm, gw), lambda i: (i, 0))
    return pl.pallas_call(
        _merge_kernel,
        out_shape=jax.ShapeDtypeStruct((t, len(outs) * gw), BF16),
        grid=(t // tm,),
        in_specs=[spec] * 6,
        out_specs=pl.BlockSpec((tm, len(outs) * gw), lambda i: (i, 0)),
        compiler_params=_params(("parallel",)),
        name="merge_dilated",
    )(*outs, *lses)


def _mla_pre_kernel(h_ref, ssq_ref, w1_ref, gq_ref, gkv_ref, wuq_ref, wukv_ref, cos_ref, sin_ref, q_ref, k_ref, v_ref):
    c = jnp.dot(h_ref[...], w1_ref[...], preferred_element_type=F32) * _row_scale(ssq_ref, h_ref.shape[1])
    cq = c[:, :MLA_Q_RANK]
    ckv = c[:, MLA_Q_RANK : MLA_Q_RANK + KV_PAD]
    kr = c[:, MLA_Q_RANK + KV_PAD :]
    cos, sin = cos_ref[...], sin_ref[...]

    nq = (cq * lax.rsqrt(jnp.mean(cq * cq, axis=-1, keepdims=True) + NORM_EPS)) * gq_ref[...]
    q = jnp.dot(nq.astype(BF16), wuq_ref[...], preferred_element_type=F32)
    ms = jnp.sum(ckv * ckv, axis=-1, keepdims=True) * (1.0 / MLA_KV_RANK)
    nkv = (ckv * lax.rsqrt(ms + NORM_EPS)) * gkv_ref[...]
    kv = jnp.dot(nkv.astype(BF16), wukv_ref[...], preferred_element_type=F32)
    k_rope = _rope(kr, cos, sin).astype(k_ref.dtype)
    for h in range(MLA_HEADS):
        lo = h * MLA_QK
        q_ref[:, lo : lo + LANE] = q[:, lo : lo + LANE].astype(q_ref.dtype)
        q_ref[:, lo + LANE : lo + MLA_QK] = _rope(q[:, lo + LANE : lo + MLA_QK], cos, sin).astype(q_ref.dtype)
        k_ref[:, lo : lo + LANE] = kv[:, h * LANE : (h + 1) * LANE].astype(k_ref.dtype)
        k_ref[:, lo + LANE : lo + MLA_QK] = k_rope
    v_ref[...] = kv[:, D_OUT_W:].astype(v_ref.dtype)


def mla_pre(h, ssq, w1, gq, gkv, wuq, wukv, cos, sin, s_len):
    t, d = h.shape
    tm = min(256, s_len)
    n_pos = s_len // tm
    const = lambda shape: pl.BlockSpec(shape, lambda i: (0, 0))
    row = lambda w: pl.BlockSpec((tm, w), lambda i: (i, 0))
    return pl.pallas_call(
        _mla_pre_kernel,
        out_shape=[
            jax.ShapeDtypeStruct((t, MLA_HEADS * MLA_QK), BF16),
            jax.ShapeDtypeStruct((t, MLA_HEADS * MLA_QK), BF16),
            jax.ShapeDtypeStruct((t, D_OUT_W), BF16),
        ],
        grid=(t // tm,),
        in_specs=[
            row(d),
            row(LANE),
            const((d, MLA_IN_W)),
            const((1, MLA_Q_RANK)),
            const((1, KV_PAD)),
            const((MLA_Q_RANK, MLA_HEADS * MLA_QK)),
            const((KV_PAD, 2 * D_OUT_W)),
            pl.BlockSpec((tm, LANE), lambda i: (i % n_pos, 0)),
            pl.BlockSpec((tm, LANE), lambda i: (i % n_pos, 0)),
        ],
        out_specs=[row(MLA_HEADS * MLA_QK), row(MLA_HEADS * MLA_QK), row(D_OUT_W)],
        compiler_params=_params(("parallel",)),
        name="mla_pre",
    )(h, ssq, w1, gq, gkv, wuq, wukv, cos, sin)


def _mla_kernel(q_ref, k_ref, v_ref, o_ref, s_a, s_b, *, tq):
    n_blk = q_ref.shape[0] // tq
    c = (MLA_NOPE + MLA_ROPE) ** -0.5 * LOG2_E

    def scores(i, dst):
        q0 = pl.multiple_of(jnp.minimum(i, n_blk - 1) * tq, tq)
        dst[...] = lax.dot_general(q_ref[pl.ds(q0, tq), :], k_ref[...], _NT, preferred_element_type=F32)

    def finish(src, i):
        s = src[...] * c
        m = jnp.max(s, axis=-1, keepdims=True)
        p = jnp.exp2(s - m)
        l = jnp.sum(p, axis=-1, keepdims=True)
        o = jnp.dot(p.astype(BF16), v_ref[...], preferred_element_type=F32) / l
        o_ref[pl.ds(pl.multiple_of(i * tq, tq), tq), :] = o.astype(o_ref.dtype)

    scores(0, s_a)

    def pair(g, carry):
        i = 2 * g
        scores(i + 1, s_b)
        finish(s_a, i)
        scores(i + 2, s_a)
        finish(s_b, i + 1)
        return carry

    lax.fori_loop(0, n_blk // 2, pair, 0)


def mla_attention(q, k, v):
    b, s_len, _ = q.shape
    tq = min(512, s_len // 2)
    blk = lambda w: pl.BlockSpec((None, s_len, w), lambda bi, h: (bi, 0, h))
    return pl.pallas_call(
        functools.partial(_mla_kernel, tq=tq),
        out_shape=jax.ShapeDtypeStruct((b, s_len, D_OUT_W), BF16),
        grid=(b, MLA_HEADS),
        in_specs=[blk(MLA_QK), blk(MLA_QK), blk(LANE)],
        out_specs=blk(LANE),
        scratch_shapes=[pltpu.VMEM((tq, s_len), F32), pltpu.VMEM((tq, s_len), F32)],
        compiler_params=_params(("parallel", "parallel")),
        name="mla_attention",
    )(q, k, v)


def _pad_cols(w, width):
    return jnp.pad(w, ((0, 0),) * (w.ndim - 1) + ((0, width - w.shape[-1]),))


def _interleave_rope(w):
    z = jnp.zeros(w.shape[:-1] + (LANE // 2 - MLA_ROPE // 2,), w.dtype)
    return jnp.concatenate([w[..., : MLA_ROPE // 2], z, w[..., MLA_ROPE // 2 :], z], axis=-1)


def prepare_layer(w_in, w_uq, mla_kv_norm, w_ukv, w_out, w_up, conv_w, conv_b, w_down):
    assert w_down.shape[0] % LANE == 0, "d_ff must be a whole number of lane tiles"
    main_w = _round_up(MAIN_W, MXU_TILE)
    w_main = _pad_cols(w_in[:, :MAIN_W], main_w).astype(BF16)
    o = MAIN_W
    w_mla = jnp.concatenate(
        [
            w_in[:, o : o + MLA_Q_RANK],
            _pad_cols(w_in[:, o + MLA_Q_RANK : o + MLA_Q_RANK + MLA_KV_RANK], KV_PAD),
            _interleave_rope(w_in[:, o + MLA_Q_RANK + MLA_KV_RANK :]),
        ],
        axis=-1,
    ).astype(BF16)
    uq = w_uq.reshape(MLA_Q_RANK, MLA_HEADS, MLA_NOPE + MLA_ROPE)
    wuq = jnp.concatenate([uq[..., :MLA_NOPE], _interleave_rope(uq[..., MLA_NOPE:])], axis=-1)
    wuq = wuq.reshape(MLA_Q_RANK, MLA_HEADS * MLA_QK).astype(BF16)
    ukv = w_ukv.reshape(MLA_KV_RANK, MLA_HEADS, MLA_NOPE + MLA_V)
    wukv = jnp.concatenate(
        [ukv[..., :MLA_NOPE].reshape(MLA_KV_RANK, D_OUT_W), ukv[..., MLA_NOPE:].reshape(MLA_KV_RANK, D_OUT_W)], axis=-1
    )
    wukv = jnp.pad(wukv, ((0, KV_PAD - MLA_KV_RANK), (0, 0))).astype(BF16)
    gkv = _pad_cols(mla_kv_norm.reshape(1, MLA_KV_RANK), KV_PAD)
    return dict(
        w_main=w_main,
        w_mla=w_mla,
        wuq=wuq,
        wukv=wukv,
        gkv=gkv,
        w_out=w_out.astype(BF16),
        w_up=w_up.astype(BF16),
        conv_w=conv_w,
        conv_b=conv_b[None],
        w_down=w_down.astype(BF16),
    )


def encoder_layer(x, h, ssq, b, s_len, lw, na_bias, sink, mla_q_norm, ffn_norm, next_attn_norm, tabs):
    t, d = x.shape
    cos128, sin128, cos64, sin64 = tabs
    proj = matmul_scaled(h, lw["w_main"], ssq, BF16, tm=MXU_TILE, tn=MXU_TILE)
    proj3 = proj.reshape(b, s_len, proj.shape[1])

    oa = na_attention(proj3, na_bias)

    (ob,) = banded_attention(
        proj3, cos128, sin128, dil=1, hw=SW_WINDOW, n_heads=SW_HEADS, q_blk=QB_BLK, k_blk=KB_BLK, v_blk=VB_BLK,
        kv_group=SW_GROUP, sink=sink,
    )

    outs, lses = [], []
    for gi, (win, dil) in enumerate(DIL_PAIRS):
        o_g, lse_g = banded_attention(
            proj3, cos128, sin128, dil=dil, hw=(win // 2) // dil, n_heads=DIL_HPG, q_blk=QC_BLK + gi * DIL_HPG,
            k_blk=KC_BLK + gi * DIL_HPG, v_blk=VC_BLK + gi * DIL_HPG, kv_group=1, want_lse=True,
        )
        outs.append(o_g.reshape(t, DIL_HPG * LANE))
        lses.append(lse_g.reshape(t, DIL_HPG * LANE))
    oc = merge_dilated(outs, lses)

    q, k, v = mla_pre(
        h, ssq, lw["w_mla"], mla_q_norm.reshape(1, MLA_Q_RANK), lw["gkv"], lw["wuq"], lw["wukv"], cos64, sin64, s_len
    )
    od = mla_attention(q.reshape(b, s_len, -1), k.reshape(b, s_len, -1), v.reshape(b, s_len, -1))

    mix = jnp.concatenate([oa.reshape(t, A_W), ob.reshape(t, B_QW), oc, od.reshape(t, D_OUT_W)], axis=-1)
    x, h2, ssq2 = matmul_residual(mix, lw["w_out"], x, g_next=ffn_norm)

    u = matmul_scaled(h2, lw["w_up"], ssq2, BF16, tm=2 * MXU_TILE, tn=MXU_TILE // 2)
    act = conv_gate(u.reshape(b, s_len, u.shape[1]), lw["conv_w"], lw["conv_b"])
    act = act.reshape(t, act.shape[-1])
    half = MXU_TILE // 2
    if next_attn_norm is None:
        return matmul_residual(act, lw["w_down"], x, tm=half, tn=half), None, None
    return matmul_residual(act, lw["w_down"], x, g_next=next_attn_norm, tm=half, tn=half)


def trunk(x3, layers, na_bias, attn_norm, sink, mla_q_norm, ffn_norm, final_norm):
    b, s_len, d = x3.shape
    tabs = rope_tables(s_len, HEAD_DIM) + rope_tables(s_len, MLA_ROPE)
    x = x3.reshape(b * s_len, d)
    h, ssq = prenorm_rows(x, attn_norm[0])
    for l, lw in enumerate(layers):
        next_attn_norm = attn_norm[l + 1] if l + 1 < len(layers) else None
        x, h, ssq = encoder_layer(
            x, h, ssq, b, s_len, lw, na_bias[l], sink[l], mla_q_norm[l], ffn_norm[l], next_attn_norm, tabs
        )
    return rms_rows(x, final_norm, F32).reshape(b, s_len, d)


def kernel(x_prompt, x_sample, attn_norm, w_in, na_rpb, sink, mla_q_norm, w_uq, mla_kv_norm, w_ukv, w_out, ffn_norm, w_up, conv_w, conv_b, w_down, final_norm):
    depth = w_in.shape[0]
    layers = [
        prepare_layer(w_in[l], w_uq[l], mla_kv_norm[l], w_ukv[l], w_out[l], w_up[l], conv_w[l], conv_b[l], w_down[l])
        for l in range(depth)
    ]
    na_bias = na_bias_tables(na_rpb)
    return tuple(
        trunk(x3, layers, na_bias, attn_norm, sink, mla_q_norm, ffn_norm, final_norm) for x3 in (x_prompt, x_sample)
    )
```

```python
import functools

import numpy as np
import jax
import jax.numpy as jnp
from jax import lax
from jax.experimental import pallas as pl
from jax.experimental.pallas import tpu as pltpu

F32 = jnp.float32
BF16 = jnp.bfloat16

HEAD_DIM = 128
ROPE_THETA = 10000.0
NORM_EPS = 1e-6
NEG_INF = -1e30
GRID_W = 64
NA_HEADS = 8
NA_ROWS = 8
NA_COLS = 16
SW_HEADS = 8
SW_KV_HEADS = 2
SW_GROUP = SW_HEADS // SW_KV_HEADS
SW_WINDOW = 128
DIL_PAIRS = ((128, 1), (512, 4), (2048, 16))
DIL_HPG = 3
DIL_HEADS = DIL_HPG * len(DIL_PAIRS)
MLA_HEADS = 7
MLA_Q_RANK = 896
MLA_KV_RANK = 320
MLA_NOPE = 128
MLA_ROPE = 64
MLA_V = 128

A_W = NA_HEADS * HEAD_DIM
B_QW = SW_HEADS * HEAD_DIM
B_KW = SW_KV_HEADS * HEAD_DIM
C_W = DIL_HEADS * HEAD_DIM
D_OUT_W = MLA_HEADS * MLA_V
MAIN_W = 3 * A_W + B_QW + 2 * B_KW + 3 * C_W

LANE = 128
SUBLANE = 8
MXU_TILE = 1024
VMEM_LIMIT = 56 * 1024 * 1024

QA_BLK, KA_BLK, VA_BLK = 0, A_W // LANE, 2 * A_W // LANE
QB_BLK = 3 * A_W // LANE
KB_BLK = QB_BLK + B_QW // LANE
VB_BLK = KB_BLK + B_KW // LANE
QC_BLK = VB_BLK + B_KW // LANE
KC_BLK = QC_BLK + C_W // LANE
VC_BLK = KC_BLK + C_W // LANE

KV_PAD = 384
MLA_IN_W = MLA_Q_RANK + KV_PAD + LANE
MLA_QK = 2 * LANE

_NT = (((1,), (1,)), ((), ()))
LOG2_E = 1.4426950408889634
LN_2 = 0.6931471805599453


def _round_up(x, m):
    return (x + m - 1) // m * m


def _params(sem, vmem=VMEM_LIMIT):
    return pltpu.CompilerParams(dimension_semantics=sem, vmem_limit_bytes=vmem)


def _rms_kernel(x_ref, g_ref, o_ref):
    x = x_ref[...]
    ms = jnp.mean(x * x, axis=-1, keepdims=True)
    o_ref[...] = ((x * lax.rsqrt(ms + NORM_EPS)) * g_ref[...]).astype(o_ref.dtype)


def rms_rows(x, g, out_dtype):
    t, d = x.shape
    tm = min(256, t)
    return pl.pallas_call(
        _rms_kernel,
        out_shape=jax.ShapeDtypeStruct((t, d), out_dtype),
        grid=(t // tm,),
        in_specs=[pl.BlockSpec((tm, d), lambda i: (i, 0)), pl.BlockSpec((1, d), lambda i: (0, 0))],
        out_specs=pl.BlockSpec((tm, d), lambda i: (i, 0)),
        compiler_params=_params(("parallel",)),
        name="rms_rows",
    )(x, g.reshape(1, d))


def _row_scale(ssq_ref, d):
    return lax.rsqrt(ssq_ref[:, 0:1] * (1.0 / d) + NORM_EPS)


def _prenorm_kernel(x_ref, g_ref, h_ref, ssq_ref):
    x = x_ref[...]
    h_ref[...] = (x * g_ref[...]).astype(h_ref.dtype)
    ssq_ref[...] = jnp.broadcast_to(jnp.sum(x * x, axis=-1, keepdims=True), ssq_ref.shape)


def prenorm_rows(x, g):
    t, d = x.shape
    tm = min(256, t)
    return pl.pallas_call(
        _prenorm_kernel,
        out_shape=[jax.ShapeDtypeStruct((t, d), BF16), jax.ShapeDtypeStruct((t, LANE), F32)],
        grid=(t // tm,),
        in_specs=[pl.BlockSpec((tm, d), lambda i: (i, 0)), pl.BlockSpec((1, d), lambda i: (0, 0))],
        out_specs=[pl.BlockSpec((tm, d), lambda i: (i, 0)), pl.BlockSpec((tm, LANE), lambda i: (i, 0))],
        compiler_params=_params(("parallel",)),
        name="prenorm_rows",
    )(x, g.reshape(1, d))


def _largest_tile(n, candidates):
    return next(c for c in candidates if n % c == 0)


def _mm_scaled_kernel(a_ref, b_ref, ssq_ref, o_ref, *, d):
    acc = jnp.dot(a_ref[...], b_ref[...], preferred_element_type=F32)
    o_ref[...] = (acc * _row_scale(ssq_ref, d)).astype(o_ref.dtype)


def matmul_scaled(a, b, ssq, out_dtype, tm, tn, n_out=None):
    m, k = a.shape
    tm = min(tm, m)
    if n_out is None:
        n = b.shape[1]
        tn = _largest_tile(n, (tn, 512, 256, LANE))
    else:
        n = _round_up(n_out, tn)
        assert n <= b.shape[1]
    return pl.pallas_call(
        functools.partial(_mm_scaled_kernel, d=k),
        out_shape=jax.ShapeDtypeStruct((m, n), out_dtype),
        grid=(m // tm, n // tn),
        in_specs=[
            pl.BlockSpec((tm, k), lambda i, j: (i, 0)),
            pl.BlockSpec((k, tn), lambda i, j: (0, j)),
            pl.BlockSpec((tm, LANE), lambda i, j: (i, 0)),
        ],
        out_specs=pl.BlockSpec((tm, tn), lambda i, j: (i, j)),
        compiler_params=_params(("parallel", "parallel")),
        name="matmul_scaled",
    )(a, b, ssq)


def _mm_res_kernel(a_ref, b_ref, r_ref, o_ref):
    o_ref[...] = r_ref[...] + jnp.dot(a_ref[...], b_ref[...], preferred_element_type=F32)


def _mm_res_prenorm_kernel(a_ref, b_ref, r_ref, g_ref, o_ref, h_ref, ssq_ref):
    x = r_ref[...] + jnp.dot(a_ref[...], b_ref[...], preferred_element_type=F32)
    o_ref[...] = x
    h_ref[...] = (x * g_ref[...]).astype(h_ref.dtype)

    @pl.when(pl.program_id(1) == 0)
    def _():
        ssq_ref[...] = jnp.zeros_like(ssq_ref)

    ssq_ref[...] += jnp.broadcast_to(jnp.sum(x * x, axis=-1, keepdims=True), ssq_ref.shape)


def matmul_residual(a, b, res, g_next=None, tm=MXU_TILE, tn=MXU_TILE // 2):
    m, k = a.shape
    n = b.shape[1]
    tm, tn = min(tm, m), min(tn, n)
    in_specs = [
        pl.BlockSpec((tm, k), lambda i, j: (i, 0)),
        pl.BlockSpec((k, tn), lambda i, j: (0, j)),
        pl.BlockSpec((tm, tn), lambda i, j: (i, j)),
    ]
    tile = pl.BlockSpec((tm, tn), lambda i, j: (i, j))
    if g_next is None:
        return pl.pallas_call(
            _mm_res_kernel,
            out_shape=jax.ShapeDtypeStruct((m, n), F32),
            grid=(m // tm, n // tn),
            in_specs=in_specs,
            out_specs=tile,
            compiler_params=_params(("parallel", "parallel")),
            name="matmul_residual",
        )(a, b, res)
    return pl.pallas_call(
        _mm_res_prenorm_kernel,
        out_shape=[
            jax.ShapeDtypeStruct((m, n), F32),
            jax.ShapeDtypeStruct((m, n), BF16),
            jax.ShapeDtypeStruct((m, LANE), F32),
        ],
        grid=(m // tm, n // tn),
        in_specs=in_specs + [pl.BlockSpec((1, tn), lambda i, j: (0, j))],
        out_specs=[tile, tile, pl.BlockSpec((tm, LANE), lambda i, j: (i, 0))],
        compiler_params=_params(("parallel", "arbitrary")),
        name="matmul_residual_prenorm",
    )(a, b, res, g_next.reshape(1, n))


FF_BLOCK = 512
FF_HALF = FF_BLOCK // 2
EDGE_ROWS = 2 * SUBLANE


def _conv3_gate(prev, cur, nxt, w, b):
    y = prev * w[0:1] + cur * w[1:2] + nxt * w[2:3] + b
    gate, up = y[:, :FF_HALF], y[:, FF_HALF:]
    return (gate / (1.0 + jnp.exp2(gate * (-LOG2_E)))) * up


def _up_conv_gate_kernel(h_ref, w_ref, ssq_ref, cw_ref, cb_ref, act_ref, edge_ref, *, d):
    tm = h_ref.shape[0]
    u = jnp.dot(h_ref[...], w_ref[...], preferred_element_type=F32) * _row_scale(ssq_ref, d)
    row = lax.broadcasted_iota(jnp.int32, u.shape, 0)
    prev = jnp.where(row == 0, 0.0, pltpu.roll(u, 1, 0))
    nxt = jnp.where(row == tm - 1, 0.0, pltpu.roll(u, tm - 1, 0))
    act_ref[...] = _conv3_gate(prev, u, nxt, cw_ref[...], cb_ref[...]).astype(act_ref.dtype)
    edge_ref[0:SUBLANE, :] = u[0:SUBLANE]
    edge_ref[SUBLANE:, :] = u[tm - SUBLANE : tm]


def _edge_fix_kernel(act_ref, lo_ref, hi_ref, cw_ref, cb_ref, o_ref):
    side = pl.program_id(1)
    rows = act_ref.shape[0]
    last = side == 0
    e = EDGE_ROWS
    prev = jnp.where(last, lo_ref[e - 2 : e - 1, :], lo_ref[e - 1 : e, :])
    cur = jnp.where(last, lo_ref[e - 1 : e, :], hi_ref[0:1, :])
    nxt = jnp.where(last, hi_ref[0:1, :], hi_ref[1:2, :])
    w, b = cw_ref[...], cb_ref[...]
    pieces = []
    for j in range(w.shape[1] // FF_BLOCK):
        sl = slice(j * FF_BLOCK, (j + 1) * FF_BLOCK)
        pieces.append(_conv3_gate(prev[:, sl], cur[:, sl], nxt[:, sl], w[:, sl], b[:, sl]))
    new_row = jnp.concatenate(pieces, axis=-1)
    row = lax.broadcasted_iota(jnp.int32, act_ref.shape, 0)
    target = jnp.where(last, rows - 1, 0)
    o_ref[...] = jnp.where(row == target, new_row, act_ref[...].astype(F32)).astype(o_ref.dtype)


def up_conv_gate(h, ssq, w, conv_w, conv_b, s_len):
    t, d = h.shape
    f = w.shape[1] // 2
    tm = min(2 * MXU_TILE, s_len)
    n_i, n_j = t // tm, f // FF_HALF
    act, edges = pl.pallas_call(
        functools.partial(_up_conv_gate_kernel, d=d),
        out_shape=[jax.ShapeDtypeStruct((t, f), BF16), jax.ShapeDtypeStruct((n_i * EDGE_ROWS, 2 * f), F32)],
        grid=(n_i, n_j),
        in_specs=[
            pl.BlockSpec((tm, d), lambda i, j: (i, 0)),
            pl.BlockSpec((d, FF_BLOCK), lambda i, j: (0, j)),
            pl.BlockSpec((tm, LANE), lambda i, j: (i, 0)),
            pl.BlockSpec((3, FF_BLOCK), lambda i, j: (0, j)),
            pl.BlockSpec((1, FF_BLOCK), lambda i, j: (0, j)),
        ],
        out_specs=[
            pl.BlockSpec((tm, FF_HALF), lambda i, j: (i, j)),
            pl.BlockSpec((EDGE_ROWS, FF_BLOCK), lambda i, j: (i, j)),
        ],
        compiler_params=_params(("parallel", "parallel")),
        name="up_conv_gate",
    )(h, w, ssq, conv_w, conv_b)

    tiles_per_seq = s_len // tm
    if tiles_per_seq == 1:
        return act
    inner = tiles_per_seq - 1
    fix_rows = 2 * SUBLANE
    lo_tile = lambda bd: (bd // inner) * tiles_per_seq + bd % inner
    act_blk = lambda bd, side: ((lo_tile(bd) + 1) * (tm // fix_rows) - 1 + side, 0)
    return pl.pallas_call(
        _edge_fix_kernel,
        out_shape=jax.ShapeDtypeStruct((t, f), BF16),
        grid=((t // s_len) * inner, 2),
        in_specs=[
            pl.BlockSpec((fix_rows, f), act_blk),
            pl.BlockSpec((EDGE_ROWS, 2 * f), lambda bd, side: (lo_tile(bd), 0)),
            pl.BlockSpec((EDGE_ROWS, 2 * f), lambda bd, side: (lo_tile(bd) + 1, 0)),
            pl.BlockSpec((3, 2 * f), lambda bd, side: (0, 0)),
            pl.BlockSpec((1, 2 * f), lambda bd, side: (0, 0)),
        ],
        out_specs=pl.BlockSpec((fix_rows, f), act_blk),
        input_output_aliases={0: 0},
        compiler_params=_params(("arbitrary", "arbitrary")),
        name="conv_edge_fix",
    )(act, edges, edges, conv_w, conv_b)


def _rope(x, cos, sin_signed):
    return x * cos + pltpu.roll(x, LANE // 2, 1) * sin_signed


def rope_tables(s_len, d):
    inv = 1.0 / (ROPE_THETA ** (jnp.arange(0, d, 2, dtype=F32) / d))
    ang = jnp.arange(s_len, dtype=F32)[:, None] * inv[None, :]
    cos, sin = jnp.cos(ang), jnp.sin(ang)
    pad = jnp.zeros((s_len, LANE // 2 - d // 2), F32)
    return (jnp.concatenate([cos, pad, cos, pad], axis=-1), jnp.concatenate([-sin, pad, sin, pad], axis=-1))


NA_GROUP = 4
NA_WIN_ROWS = 12
NA_FIRST, NA_INTERIOR, NA_LAST = 0, 1, 2


def _na_kernel(q_ref, k_ref, v_ref, bias_ref, o_ref, *, rows, unroll):
    scale = HEAD_DIM**-0.5
    n_groups = rows // NA_GROUP
    gq = NA_GROUP * GRID_W
    wk = NA_WIN_ROWS * GRID_W

    def one_group(g):
        r0 = g * NA_GROUP
        ws = jnp.clip(r0 - NA_ROWS // 2, 0, rows - NA_WIN_ROWS)
        variant = jnp.where(g == 0, NA_FIRST, jnp.where(g == n_groups - 1, NA_LAST, NA_INTERIOR))
        q0 = pl.multiple_of(r0 * GRID_W, gq)
        k0 = pl.multiple_of(ws * GRID_W, GRID_W)
        s = lax.dot_general(q_ref[pl.ds(q0, gq), :], k_ref[pl.ds(k0, wk), :], _NT, preferred_element_type=F32)
        s = s * (scale * LOG2_E) + bias_ref[variant]
        m = jnp.max(s, axis=-1, keepdims=True)
        p = jnp.exp2(s - m)
        l = jnp.sum(p, axis=-1, keepdims=True)
        o = jnp.dot(p.astype(BF16), v_ref[pl.ds(k0, wk), :], preferred_element_type=F32) / l
        o_ref[pl.ds(q0, gq), :] = o.astype(o_ref.dtype)

    def body(it, carry):
        for u in range(unroll):
            one_group(it * unroll + u)
        return carry

    lax.fori_loop(0, n_groups // unroll, body, 0)


def na_bias_tables(rpb_all):
    qc = np.arange(GRID_W)[:, None]
    kc = np.arange(GRID_W)[None, :]
    q_ws = np.clip(qc - NA_COLS // 2, 0, GRID_W - NA_COLS)
    valid = (kc >= q_ws) & (kc < q_ws + NA_COLS)
    coff = np.clip(kc - qc + NA_COLS - 1, 0, 2 * NA_COLS - 2)
    onehot = (coff[None] == np.arange(2 * NA_COLS - 1)[:, None, None]).astype(np.float32)
    e = jnp.einsum("lhrc,cqk->lhrqk", rpb_all.astype(F32), onehot, precision=lax.Precision.HIGHEST)
    e = jnp.where(valid, e * LOG2_E, NEG_INF)
    pad = NA_WIN_ROWS
    e_pad = jnp.pad(e, ((0, 0), (0, 0), (pad, pad), (0, 0), (0, 0)))
    i = np.arange(NA_WIN_ROWS)
    slabs, valid_rows = [], []
    for r0_minus_ws, first_valid in ((0, lambda u: 0), (NA_ROWS // 2, lambda u: u), (NA_ROWS, lambda u: NA_ROWS // 2)):
        for u in range(NA_GROUP):
            lo = pad + (NA_ROWS - 1) - u - r0_minus_ws
            slabs.append(e_pad[:, :, lo : lo + NA_WIN_ROWS])
            valid_rows.append((i >= first_valid(u)) & (i < first_valid(u) + NA_ROWS))
    tab = jnp.stack(slabs, axis=2)
    valid_rows = np.stack(valid_rows)[:, :, None, None]
    tab = jnp.where(valid_rows, tab, NEG_INF)
    tab = tab.transpose(0, 1, 2, 4, 3, 5)
    return tab.reshape(tab.shape[:2] + (3, NA_GROUP * GRID_W, NA_WIN_ROWS * GRID_W))


def na_attention(proj3, bias_tab):
    b, s_len, _ = proj3.shape
    rows = s_len // GRID_W
    assert rows >= NA_WIN_ROWS and rows % NA_GROUP == 0, "grid too short for the grouped neighbourhood kernel"
    n_groups = rows // NA_GROUP
    blk = lambda off: pl.BlockSpec((None, s_len, LANE), lambda bi, h, off=off: (bi, 0, off + h))
    return pl.pallas_call(
        functools.partial(_na_kernel, rows=rows, unroll=4 if n_groups % 4 == 0 else 1),
        out_shape=jax.ShapeDtypeStruct((b, s_len, A_W), BF16),
        grid=(b, NA_HEADS),
        in_specs=[
            blk(QA_BLK),
            blk(KA_BLK),
            blk(VA_BLK),
            pl.BlockSpec((None, 3, NA_GROUP * GRID_W, NA_WIN_ROWS * GRID_W), lambda bi, h: (h, 0, 0, 0)),
        ],
        out_specs=pl.BlockSpec((None, s_len, LANE), lambda bi, h: (bi, 0, h)),
        compiler_params=_params(("parallel", "parallel")),
        name="na_attention",
    )(proj3, proj3, proj3, bias_tab)


def _band_kernel(*refs, s_len, dil, tq, hw, win, has_sink, want_lse, unroll):
    refs = list(refs)
    q_ref, k_ref, v_ref, cos_ref, sin_ref = refs[:5]
    del refs[:5]
    sink_ref = refs.pop(0) if has_sink else None
    o_ref = refs.pop(0)
    lse_ref = refs.pop(0) if want_lse else None
    stage, qd, kd, vd = refs[:4]
    od = refs[4] if dil > 1 else None
    ld = refs[5] if (dil > 1 and want_lse) else None
    sub = s_len // dil
    nq = sub // tq
    scale = HEAD_DIM**-0.5

    def regroup(val, dst):
        if dil == 1:
            dst[0] = val.astype(BF16)
        else:
            stage[...] = val
            for r in range(dil):
                dst[r] = stage[pl.ds(r, sub, stride=dil), :].astype(BF16)

    cos, sin = cos_ref[...], sin_ref[...]
    regroup(_rope(k_ref[...].astype(F32), cos, sin), kd)
    regroup(_rope(q_ref[...].astype(F32), cos, sin), qd)
    regroup(v_ref[...].astype(F32), vd)

    def one_block(it):
        r = it // nq
        q0 = pl.multiple_of((it % nq) * tq, tq)
        ws = pl.multiple_of(jnp.clip(q0 - hw, 0, sub - win), 64)
        s = lax.dot_general(qd[r, pl.ds(q0, tq), :], kd[r, pl.ds(ws, win), :], _NT, preferred_element_type=F32)
        s = s * (scale * LOG2_E)
        qpos = q0 + lax.broadcasted_iota(jnp.int32, (tq, win), 0)
        kpos = ws + lax.broadcasted_iota(jnp.int32, (tq, win), 1)
        s = jnp.where(jnp.abs(kpos - qpos) <= hw, s, NEG_INF)
        m = jnp.max(s, axis=-1, keepdims=True)
        if has_sink:
            sk = sink_ref[0:1, 0:1] * LOG2_E
            m = jnp.maximum(m, sk)
        p = jnp.exp2(s - m)
        l = jnp.sum(p, axis=-1, keepdims=True)
        if has_sink:
            l = l + jnp.exp2(sk - m)
        o = jnp.dot(p.astype(BF16), vd[r, pl.ds(ws, win), :], preferred_element_type=F32) / l
        if dil == 1:
            o_ref[pl.ds(q0, tq), :] = o.astype(o_ref.dtype)
        else:
            od[r, pl.ds(q0, tq), :] = o
        if want_lse:
            lse = jnp.broadcast_to(m * LN_2 + jnp.log(l), (tq, LANE))
            if dil == 1:
                lse_ref[pl.ds(q0, tq), :] = lse
            else:
                ld[r, pl.ds(q0, tq), :] = lse

    def body(g, carry):
        for u in range(unroll):
            one_block(g * unroll + u)
        return carry

    lax.fori_loop(0, dil * nq // unroll, body, 0)

    if dil > 1:
        for r in range(dil):
            stage[pl.ds(r, sub, stride=dil), :] = od[r]
        o_ref[...] = stage[...].astype(o_ref.dtype)
        if want_lse:
            for r in range(dil):
                lse_ref[pl.ds(r, sub, stride=dil), :] = ld[r]


def banded_attention(proj3, cos, sin, *, dil, hw, n_heads, q_blk, k_blk, v_blk, kv_group, sink=None, want_lse=False):
    b, s_len, _ = proj3.shape
    sub = s_len // dil
    tq = min(max(128, 2 * hw), sub)
    win = min(tq + 2 * hw, sub)
    total = dil * (sub // tq)
    unroll = 4 if total % 4 == 0 else 1
    out_w = n_heads * LANE

    def col(off, group=1):
        return pl.BlockSpec((None, s_len, LANE), lambda bi, h: (bi, 0, off + h // group))

    table = pl.BlockSpec((s_len, LANE), lambda bi, h: (0, 0))
    in_specs = [col(q_blk), col(k_blk, kv_group), col(v_blk, kv_group), table, table]
    args = [proj3, proj3, proj3, cos, sin]
    if sink is not None:
        in_specs.append(pl.BlockSpec((None, SUBLANE, LANE), lambda bi, h: (h, 0, 0)))
        args.append(jnp.broadcast_to(sink.astype(F32)[:, None, None], (n_heads, SUBLANE, LANE)))
    out_spec = pl.BlockSpec((None, s_len, LANE), lambda bi, h: (bi, 0, h))
    out_shape = [jax.ShapeDtypeStruct((b, s_len, out_w), BF16)]
    out_specs = [out_spec]
    if want_lse:
        out_shape.append(jax.ShapeDtypeStruct((b, s_len, out_w), F32))
        out_specs.append(out_spec)
    slab = (dil, sub, LANE)
    scratch = [pltpu.VMEM((s_len, LANE), F32), pltpu.VMEM(slab, BF16), pltpu.VMEM(slab, BF16), pltpu.VMEM(slab, BF16)]
    if dil > 1:
        scratch.append(pltpu.VMEM(slab, F32))
        if want_lse:
            scratch.append(pltpu.VMEM(slab, F32))
    return pl.pallas_call(
        functools.partial(
            _band_kernel, s_len=s_len, dil=dil, tq=tq, hw=hw, win=win, has_sink=sink is not None,
            want_lse=want_lse, unroll=unroll,
        ),
        out_shape=out_shape,
        grid=(b, n_heads),
        in_specs=in_specs,
        out_specs=out_specs,
        scratch_shapes=scratch,
        compiler_params=_params(("parallel", "arbitrary")),
        name=f"banded_attention_d{dil}",
    )(*args)


def _merge_kernel(o0_ref, o1_ref, o2_ref, l0_ref, l1_ref, l2_ref, out_ref):
    l0, l1, l2 = l0_ref[...], l1_ref[...], l2_ref[...]
    mx = jnp.maximum(jnp.maximum(l0, l1), l2)
    e0, e1, e2 = jnp.exp(l0 - mx), jnp.exp(l1 - mx), jnp.exp(l2 - mx)
    den = e0 + e1 + e2
    gw = DIL_HPG * LANE
    for g, (o_ref, e) in enumerate(((o0_ref, e0), (o1_ref, e1), (o2_ref, e2))):
        out_ref[:, g * gw : (g + 1) * gw] = (o_ref[...].astype(F32) * (e / den)).astype(out_ref.dtype)


def merge_dilated(outs, lses):
    t, gw = outs[0].shape
    tm = min(512, t)
    spec = pl.BlockSpec((tm, gw), lambda i: (i, 0))
    return pl.pallas_call(
        _merge_kernel,
        out_shape=jax.ShapeDtypeStruct((t, len(outs) * gw), BF16),
        grid=(t // tm,),
        in_specs=[spec] * 6,
        out_specs=pl.BlockSpec((tm, len(outs) * gw), lambda i: (i, 0)),
        compiler_params=_params(("parallel",)),
        name="merge_dilated",
    )(*outs, *lses)


def _mla_pre_kernel(h_ref, ssq_ref, w1_ref, gq_ref, gkv_ref, wuq_ref, wukv_ref, cos_ref, sin_ref, q_ref, k_ref, v_ref):
    c = jnp.dot(h_ref[...], w1_ref[...], preferred_element_type=F32) * _row_scale(ssq_ref, h_ref.shape[1])
    cq = c[:, :MLA_Q_RANK]
    ckv = c[:, MLA_Q_RANK : MLA_Q_RANK + KV_PAD]
    kr = c[:, MLA_Q_RANK + KV_PAD :]
    cos, sin = cos_ref[...], sin_ref[...]

    nq = (cq * lax.rsqrt(jnp.mean(cq * cq, axis=-1, keepdims=True) + NORM_EPS)) * gq_ref[...]
    q = jnp.dot(nq.astype(BF16), wuq_ref[...], preferred_element_type=F32)
    ms = jnp.sum(ckv * ckv, axis=-1, keepdims=True) * (1.0 / MLA_KV_RANK)
    nkv = (ckv * lax.rsqrt(ms + NORM_EPS)) * gkv_ref[...]
    kv = jnp.dot(nkv.astype(BF16), wukv_ref[...], preferred_element_type=F32)
    k_rope = _rope(kr, cos, sin).astype(k_ref.dtype)
    for h in range(MLA_HEADS):
        lo = h * MLA_QK
        q_ref[:, lo : lo + LANE] = q[:, lo : lo + LANE].astype(q_ref.dtype)
        q_ref[:, lo + LANE : lo + MLA_QK] = _rope(q[:, lo + LANE : lo + MLA_QK], cos, sin).astype(q_ref.dtype)
        k_ref[:, lo : lo + LANE] = kv[:, h * LANE : (h + 1) * LANE].astype(k_ref.dtype)
        k_ref[:, lo + LANE : lo + MLA_QK] = k_rope
    v_ref[...] = kv[:, D_OUT_W:].astype(v_ref.dtype)


def mla_pre(h, ssq, w1, gq, gkv, wuq, wukv, cos, sin, s_len):
    t, d = h.shape
    tm = min(256, s_len)
    n_pos = s_len // tm
    const = lambda shape: pl.BlockSpec(shape, lambda i: (0, 0))
    row = lambda w: pl.BlockSpec((tm, w), lambda i: (i, 0))
    return pl.pallas_call(
        _mla_pre_kernel,
        out_shape=[
            jax.ShapeDtypeStruct((t, MLA_HEADS * MLA_QK), BF16),
            jax.ShapeDtypeStruct((t, MLA_HEADS * MLA_QK), BF16),
            jax.ShapeDtypeStruct((t, D_OUT_W), BF16),
        ],
        grid=(t // tm,),
        in_specs=[
            row(d),
            row(LANE),
            const((d, MLA_IN_W)),
            const((1, MLA_Q_RANK)),
            const((1, KV_PAD)),
            const((MLA_Q_RANK, MLA_HEADS * MLA_QK)),
            const((KV_PAD, 2 * D_OUT_W)),
            pl.BlockSpec((tm, LANE), lambda i: (i % n_pos, 0)),
            pl.BlockSpec((tm, LANE), lambda i: (i % n_pos, 0)),
        ],
        out_specs=[row(MLA_HEADS * MLA_QK), row(MLA_HEADS * MLA_QK), row(D_OUT_W)],
        compiler_params=_params(("parallel",)),
        name="mla_pre",
    )(h, ssq, w1, gq, gkv, wuq, wukv, cos, sin)


def _mla_kernel(q_ref, k_ref, v_ref, o_ref, s_a, s_b, *, tq):
    n_blk = q_ref.shape[0] // tq
    c = (MLA_NOPE + MLA_ROPE) ** -0.5 * LOG2_E

    def scores(i, dst):
        q0 = pl.multiple_of(jnp.minimum(i, n_blk - 1) * tq, tq)
        dst[...] = lax.dot_general(q_ref[pl.ds(q0, tq), :], k_ref[...], _NT, preferred_element_type=F32)

    def finish(src, i):
        s = src[...] * c
        m = jnp.max(s, axis=-1, keepdims=True)
        p = jnp.exp2(s - m)
        l = jnp.sum(p, axis=-1, keepdims=True)
        o = jnp.dot(p.astype(BF16), v_ref[...], preferred_element_type=F32) / l
        o_ref[pl.ds(pl.multiple_of(i * tq, tq), tq), :] = o.astype(o_ref.dtype)

    scores(0, s_a)

    def pair(g, carry):
        i = 2 * g
        scores(i + 1, s_b)
        finish(s_a, i)
        scores(i + 2, s_a)
        finish(s_b, i + 1)
        return carry

    lax.fori_loop(0, n_blk // 2, pair, 0)


def mla_attention(q, k, v):
    b, s_len, _ = q.shape
    tq = min(512, s_len // 2)
    blk = lambda w: pl.BlockSpec((None, s_len, w), lambda bi, h: (bi, 0, h))
    return pl.pallas_call(
        functools.partial(_mla_kernel, tq=tq),
        out_shape=jax.ShapeDtypeStruct((b, s_len, D_OUT_W), BF16),
        grid=(b, MLA_HEADS),
        in_specs=[blk(MLA_QK), blk(MLA_QK), blk(LANE)],
        out_specs=blk(LANE),
        scratch_shapes=[pltpu.VMEM((tq, s_len), F32), pltpu.VMEM((tq, s_len), F32)],
        compiler_params=_params(("parallel", "parallel")),
        name="mla_attention",
    )(q, k, v)


def _pad_cols(w, width):
    return jnp.pad(w, ((0, 0),) * (w.ndim - 1) + ((0, width - w.shape[-1]),))


def _interleave_rope(w):
    z = jnp.zeros(w.shape[:-1] + (LANE // 2 - MLA_ROPE // 2,), w.dtype)
    return jnp.concatenate([w[..., : MLA_ROPE // 2], z, w[..., MLA_ROPE // 2 :], z], axis=-1)


def _gate_up_blocks(w):
    f = w.shape[-1] // 2
    lead = w.shape[:-1]
    gate = w[..., :f].reshape(lead + (f // FF_HALF, FF_HALF))
    up = w[..., f:].reshape(lead + (f // FF_HALF, FF_HALF))
    return jnp.stack([gate, up], axis=-2).reshape(lead + (2 * f,))


def prepare_layer(w_in, w_uq, mla_kv_norm, w_ukv, w_out, w_up, conv_w, conv_b, w_down):
    assert w_down.shape[0] % FF_HALF == 0, "d_ff must be a whole number of gate column blocks"
    w_in = w_in.astype(BF16)
    o = MAIN_W
    w_mla = jnp.concatenate(
        [
            w_in[:, o : o + MLA_Q_RANK],
            _pad_cols(w_in[:, o + MLA_Q_RANK : o + MLA_Q_RANK + MLA_KV_RANK], KV_PAD),
            _interleave_rope(w_in[:, o + MLA_Q_RANK + MLA_KV_RANK :]),
        ],
        axis=-1,
    )
    uq = w_uq.reshape(MLA_Q_RANK, MLA_HEADS, MLA_NOPE + MLA_ROPE)
    wuq = jnp.concatenate([uq[..., :MLA_NOPE], _interleave_rope(uq[..., MLA_NOPE:])], axis=-1)
    wuq = wuq.reshape(MLA_Q_RANK, MLA_HEADS * MLA_QK).astype(BF16)
    ukv = w_ukv.reshape(MLA_KV_RANK, MLA_HEADS, MLA_NOPE + MLA_V)
    wukv = jnp.concatenate(
        [ukv[..., :MLA_NOPE].reshape(MLA_KV_RANK, D_OUT_W), ukv[..., MLA_NOPE:].reshape(MLA_KV_RANK, D_OUT_W)], axis=-1
    )
    wukv = jnp.pad(wukv, ((0, KV_PAD - MLA_KV_RANK), (0, 0))).astype(BF16)
    gkv = _pad_cols(mla_kv_norm.reshape(1, MLA_KV_RANK), KV_PAD)
    return dict(
        w_in=w_in,
        w_mla=w_mla,
        wuq=wuq,
        wukv=wukv,
        gkv=gkv,
        w_out=w_out.astype(BF16),
        w_up=_gate_up_blocks(w_up.astype(BF16)),
        conv_w=_gate_up_blocks(conv_w),
        conv_b=_gate_up_blocks(conv_b[None]),
        w_down=w_down.astype(BF16),
    )


def encoder_layer(x, h, ssq, b, s_len, lw, na_bias, sink, mla_q_norm, ffn_norm, next_attn_norm, tabs):
    t, d = x.shape
    cos128, sin128, cos64, sin64 = tabs
    proj = matmul_scaled(h, lw["w_in"], ssq, BF16, tm=MXU_TILE, tn=MXU_TILE, n_out=MAIN_W)
    proj3 = proj.reshape(b, s_len, proj.shape[1])

    oa = na_attention(proj3, na_bias)

    (ob,) = banded_attention(
        proj3, cos128, sin128, dil=1, hw=SW_WINDOW, n_heads=SW_HEADS, q_blk=QB_BLK, k_blk=KB_BLK, v_blk=VB_BLK,
        kv_group=SW_GROUP, sink=sink,
    )

    outs, lses = [], []
    for gi, (win, dil) in enumerate(DIL_PAIRS):
        o_g, lse_g = banded_attention(
            proj3, cos128, sin128, dil=dil, hw=(win // 2) // dil, n_heads=DIL_HPG, q_blk=QC_BLK + gi * DIL_HPG,
            k_blk=KC_BLK + gi * DIL_HPG, v_blk=VC_BLK + gi * DIL_HPG, kv_group=1, want_lse=True,
        )
        outs.append(o_g.reshape(t, DIL_HPG * LANE))
        lses.append(lse_g.reshape(t, DIL_HPG * LANE))
    oc = merge_dilated(outs, lses)

    q, k, v = mla_pre(
        h, ssq, lw["w_mla"], mla_q_norm.reshape(1, MLA_Q_RANK), lw["gkv"], lw["wuq"], lw["wukv"], cos64, sin64, s_len
    )
    od = mla_attention(q.reshape(b, s_len, -1), k.reshape(b, s_len, -1), v.reshape(b, s_len, -1))

    mix = jnp.concatenate([oa.reshape(t, A_W), ob.reshape(t, B_QW), oc, od.reshape(t, D_OUT_W)], axis=-1)
    x, h2, ssq2 = matmul_residual(mix, lw["w_out"], x, g_next=ffn_norm)

    act = up_conv_gate(h2, ssq2, lw["w_up"], lw["conv_w"], lw["conv_b"], s_len)
    half = MXU_TILE // 2
    if next_attn_norm is None:
        return matmul_residual(act, lw["w_down"], x, tm=half, tn=half), None, None
    return matmul_residual(act, lw["w_down"], x, g_next=next_attn_norm, tm=half, tn=half)


def trunk(x3, layers, na_bias, attn_norm, sink, mla_q_norm, ffn_norm, final_norm):
    b, s_len, d = x3.shape
    tabs = rope_tables(s_len, HEAD_DIM) + rope_tables(s_len, MLA_ROPE)
    x = x3.reshape(b * s_len, d)
    h, ssq = prenorm_rows(x, attn_norm[0])
    for l, lw in enumerate(layers):
        next_attn_norm = attn_norm[l + 1] if l + 1 < len(layers) else None
        x, h, ssq = encoder_layer(
            x, h, ssq, b, s_len, lw, na_bias[l], sink[l], mla_q_norm[l], ffn_norm[l], next_attn_norm, tabs
        )
    return rms_rows(x, final_norm, F32).reshape(b, s_len, d)


def kernel(x_prompt, x_sample, attn_norm, w_in, na_rpb, sink, mla_q_norm, w_uq, mla_kv_norm, w_ukv, w_out, ffn_norm, w_up, conv_w, conv_b, w_down, final_norm):
    depth = w_in.shape[0]
    layers = [
        prepare_layer(w_in[l], w_uq[l], mla_kv_norm[l], w_ukv[l], w_out[l], w_up[l], conv_w[l], conv_b[l], w_down[l])
        for l in range(depth)
    ]
    na_bias = na_bias_tables(na_rpb)
    return tuple(
        trunk(x3, layers, na_bias, attn_norm, sink, mla_q_norm, ffn_norm, final_norm) for x3 in (x_prompt, x_sample)
    )
```

```python
import functools

import numpy as np
import jax
import jax.numpy as jnp
from jax import lax
from jax.experimental import pallas as pl
from jax.experimental.pallas import tpu as pltpu

F32 = jnp.float32
BF16 = jnp.bfloat16

HEAD_DIM = 128
ROPE_THETA = 10000.0
NORM_EPS = 1e-6
NEG_INF = -1e30
GRID_W = 64
NA_HEADS = 8
NA_ROWS = 8
NA_COLS = 16
SW_HEADS = 8
SW_KV_HEADS = 2
SW_GROUP = SW_HEADS // SW_KV_HEADS
SW_WINDOW = 128
DIL_PAIRS = ((128, 1), (512, 4), (2048, 16))
DIL_HPG = 3
DIL_HEADS = DIL_HPG * len(DIL_PAIRS)
MLA_HEADS = 7
MLA_Q_RANK = 896
MLA_KV_RANK = 320
MLA_NOPE = 128
MLA_ROPE = 64
MLA_V = 128

A_W = NA_HEADS * HEAD_DIM
B_QW = SW_HEADS * HEAD_DIM
B_KW = SW_KV_HEADS * HEAD_DIM
C_W = DIL_HEADS * HEAD_DIM
D_OUT_W = MLA_HEADS * MLA_V
MAIN_W = 3 * A_W + B_QW + 2 * B_KW + 3 * C_W

LANE = 128
SUBLANE = 8
MXU_TILE = 1024
VMEM_LIMIT = 56 * 1024 * 1024

QA_BLK, KA_BLK, VA_BLK = 0, A_W // LANE, 2 * A_W // LANE
QB_BLK = 3 * A_W // LANE
KB_BLK = QB_BLK + B_QW // LANE
VB_BLK = KB_BLK + B_KW // LANE
QC_BLK = VB_BLK + B_KW // LANE
KC_BLK = QC_BLK + C_W // LANE
VC_BLK = KC_BLK + C_W // LANE

KV_PAD = 384
MLA_IN_W = MLA_Q_RANK + KV_PAD + LANE
MLA_QK = 2 * LANE

_NT = (((1,), (1,)), ((), ()))
LOG2_E = 1.4426950408889634
LN_2 = 0.6931471805599453


def _round_up(x, m):
    return (x + m - 1) // m * m


def _params(sem, vmem=VMEM_LIMIT):
    return pltpu.CompilerParams(dimension_semantics=sem, vmem_limit_bytes=vmem)


def _rms_kernel(x_ref, g_ref, o_ref):
    x = x_ref[...]
    ms = jnp.mean(x * x, axis=-1, keepdims=True)
    o_ref[...] = ((x * lax.rsqrt(ms + NORM_EPS)) * g_ref[...]).astype(o_ref.dtype)


def rms_rows(x, g, out_dtype):
    t, d = x.shape
    tm = min(256, t)
    return pl.pallas_call(
        _rms_kernel,
        out_shape=jax.ShapeDtypeStruct((t, d), out_dtype),
        grid=(t // tm,),
        in_specs=[pl.BlockSpec((tm, d), lambda i: (i, 0)), pl.BlockSpec((1, d), lambda i: (0, 0))],
        out_specs=pl.BlockSpec((tm, d), lambda i: (i, 0)),
        compiler_params=_params(("parallel",)),
        name="rms_rows",
    )(x, g.reshape(1, d))


def _row_scale(ssq_ref, d):
    return lax.rsqrt(ssq_ref[:, 0:1] * (1.0 / d) + NORM_EPS)


def _prenorm_kernel(x_ref, g_ref, h_ref, ssq_ref):
    x = x_ref[...]
    h_ref[...] = (x * g_ref[...]).astype(h_ref.dtype)
    ssq_ref[...] = jnp.broadcast_to(jnp.sum(x * x, axis=-1, keepdims=True), ssq_ref.shape)


def prenorm_rows(x, g):
    t, d = x.shape
    tm = min(256, t)
    return pl.pallas_call(
        _prenorm_kernel,
        out_shape=[jax.ShapeDtypeStruct((t, d), BF16), jax.ShapeDtypeStruct((t, LANE), F32)],
        grid=(t // tm,),
        in_specs=[pl.BlockSpec((tm, d), lambda i: (i, 0)), pl.BlockSpec((1, d), lambda i: (0, 0))],
        out_specs=[pl.BlockSpec((tm, d), lambda i: (i, 0)), pl.BlockSpec((tm, LANE), lambda i: (i, 0))],
        compiler_params=_params(("parallel",)),
        name="prenorm_rows",
    )(x, g.reshape(1, d))


def _largest_tile(n, candidates):
    return next(c for c in candidates if n % c == 0)


def _mm_scaled_kernel(a_ref, b_ref, ssq_ref, o_ref, *, d):
    acc = jnp.dot(a_ref[...], b_ref[...], preferred_element_type=F32)
    o_ref[...] = (acc * _row_scale(ssq_ref, d)).astype(o_ref.dtype)


def matmul_scaled(a, b, ssq, out_dtype, tm, tn, n_out=None):
    m, k = a.shape
    tm = min(tm, m)
    if n_out is None:
        n = b.shape[1]
        tn = _largest_tile(n, (tn, 512, 256, LANE))
    else:
        n = _round_up(n_out, tn)
        assert n <= b.shape[1]
    return pl.pallas_call(
        functools.partial(_mm_scaled_kernel, d=k),
        out_shape=jax.ShapeDtypeStruct((m, n), out_dtype),
        grid=(m // tm, n // tn),
        in_specs=[
            pl.BlockSpec((tm, k), lambda i, j: (i, 0)),
            pl.BlockSpec((k, tn), lambda i, j: (0, j)),
            pl.BlockSpec((tm, LANE), lambda i, j: (i, 0)),
        ],
        out_specs=pl.BlockSpec((tm, tn), lambda i, j: (i, j)),
        compiler_params=_params(("parallel", "parallel")),
        name="matmul_scaled",
    )(a, b, ssq)


def _mm_res_kernel(a_ref, b_ref, r_ref, o_ref):
    o_ref[...] = r_ref[...] + jnp.dot(a_ref[...], b_ref[...], preferred_element_type=F32)


def _mm_res_prenorm_kernel(a_ref, b_ref, r_ref, g_ref, o_ref, h_ref, ssq_ref):
    x = r_ref[...] + jnp.dot(a_ref[...], b_ref[...], preferred_element_type=F32)
    o_ref[...] = x
    h_ref[...] = (x * g_ref[...]).astype(h_ref.dtype)

    @pl.when(pl.program_id(1) == 0)
    def _():
        ssq_ref[...] = jnp.zeros_like(ssq_ref)

    ssq_ref[...] += jnp.broadcast_to(jnp.sum(x * x, axis=-1, keepdims=True), ssq_ref.shape)


def matmul_residual(a, b, res, g_next=None, tm=MXU_TILE, tn=MXU_TILE // 2):
    m, k = a.shape
    n = b.shape[1]
    tm, tn = min(tm, m), min(tn, n)
    in_specs = [
        pl.BlockSpec((tm, k), lambda i, j: (i, 0)),
        pl.BlockSpec((k, tn), lambda i, j: (0, j)),
        pl.BlockSpec((tm, tn), lambda i, j: (i, j)),
    ]
    tile = pl.BlockSpec((tm, tn), lambda i, j: (i, j))
    if g_next is None:
        return pl.pallas_call(
            _mm_res_kernel,
            out_shape=jax.ShapeDtypeStruct((m, n), F32),
            grid=(m // tm, n // tn),
            in_specs=in_specs,
            out_specs=tile,
            compiler_params=_params(("parallel", "parallel")),
            name="matmul_residual",
        )(a, b, res)
    return pl.pallas_call(
        _mm_res_prenorm_kernel,
        out_shape=[
            jax.ShapeDtypeStruct((m, n), F32),
            jax.ShapeDtypeStruct((m, n), BF16),
            jax.ShapeDtypeStruct((m, LANE), F32),
        ],
        grid=(m // tm, n // tn),
        in_specs=in_specs + [pl.BlockSpec((1, tn), lambda i, j: (0, j))],
        out_specs=[tile, tile, pl.BlockSpec((tm, LANE), lambda i, j: (i, 0))],
        compiler_params=_params(("parallel", "arbitrary")),
        name="matmul_residual_prenorm",
    )(a, b, res, g_next.reshape(1, n))


FF_BLOCK = 512
FF_HALF = FF_BLOCK // 2
EDGE_ROWS = 2 * SUBLANE


def _conv3_gate(prev, cur, nxt, w, b):
    y = prev * w[0:1] + cur * w[1:2] + nxt * w[2:3] + b
    gate, up = y[:, :FF_HALF], y[:, FF_HALF:]
    return (gate / (1.0 + jnp.exp2(gate * (-LOG2_E)))) * up


def _up_conv_gate_kernel(h_ref, wg_ref, wu_ref, ssq_ref, cwg_ref, cwu_ref, cbg_ref, cbu_ref, act_ref, edge_ref, *, d):
    tm = h_ref.shape[0]
    scale = _row_scale(ssq_ref, d)
    row = lax.broadcasted_iota(jnp.int32, (tm, FF_HALF), 0)

    def conv_branch(w_ref, cw_ref, cb_ref, lane0):
        u = jnp.dot(h_ref[...], w_ref[...], preferred_element_type=F32) * scale
        edge_ref[0:SUBLANE, lane0 : lane0 + FF_HALF] = u[0:SUBLANE]
        edge_ref[SUBLANE:, lane0 : lane0 + FF_HALF] = u[tm - SUBLANE : tm]
        prev = jnp.where(row == 0, 0.0, pltpu.roll(u, 1, 0))
        nxt = jnp.where(row == tm - 1, 0.0, pltpu.roll(u, tm - 1, 0))
        w = cw_ref[...]
        return prev * w[0:1] + u * w[1:2] + nxt * w[2:3] + cb_ref[...]

    gate = conv_branch(wg_ref, cwg_ref, cbg_ref, 0)
    up = conv_branch(wu_ref, cwu_ref, cbu_ref, FF_HALF)
    act_ref[...] = ((gate / (1.0 + jnp.exp2(gate * (-LOG2_E)))) * up).astype(act_ref.dtype)


def _edge_fix_kernel(act_ref, lo_ref, hi_ref, cw_ref, cb_ref, o_ref):
    side = pl.program_id(1)
    rows = act_ref.shape[0]
    last = side == 0
    e = EDGE_ROWS
    prev = jnp.where(last, lo_ref[e - 2 : e - 1, :], lo_ref[e - 1 : e, :])
    cur = jnp.where(last, lo_ref[e - 1 : e, :], hi_ref[0:1, :])
    nxt = jnp.where(last, hi_ref[0:1, :], hi_ref[1:2, :])
    w, b = cw_ref[...], cb_ref[...]
    pieces = []
    for j in range(w.shape[1] // FF_BLOCK):
        sl = slice(j * FF_BLOCK, (j + 1) * FF_BLOCK)
        pieces.append(_conv3_gate(prev[:, sl], cur[:, sl], nxt[:, sl], w[:, sl], b[:, sl]))
    new_row = jnp.concatenate(pieces, axis=-1)
    row = lax.broadcasted_iota(jnp.int32, act_ref.shape, 0)
    target = jnp.where(last, rows - 1, 0)
    o_ref[...] = jnp.where(row == target, new_row, act_ref[...].astype(F32)).astype(o_ref.dtype)


def up_conv_gate(h, ssq, w, conv_w, conv_b, s_len):
    t, d = h.shape
    f = w.shape[1] // 2
    tm = min(2 * MXU_TILE, s_len)
    n_i, n_j = t // tm, f // FF_HALF
    act, edges = pl.pallas_call(
        functools.partial(_up_conv_gate_kernel, d=d),
        out_shape=[jax.ShapeDtypeStruct((t, f), BF16), jax.ShapeDtypeStruct((n_i * EDGE_ROWS, 2 * f), F32)],
        grid=(n_i, n_j),
        in_specs=[
            pl.BlockSpec((tm, d), lambda i, j: (i, 0)),
            pl.BlockSpec((d, FF_HALF), lambda i, j: (0, j)),
            pl.BlockSpec((d, FF_HALF), lambda i, j: (0, j + n_j)),
            pl.BlockSpec((tm, LANE), lambda i, j: (i, 0)),
            pl.BlockSpec((3, FF_HALF), lambda i, j: (0, j)),
            pl.BlockSpec((3, FF_HALF), lambda i, j: (0, j + n_j)),
            pl.BlockSpec((1, FF_HALF), lambda i, j: (0, j)),
            pl.BlockSpec((1, FF_HALF), lambda i, j: (0, j + n_j)),
        ],
        out_specs=[
            pl.BlockSpec((tm, FF_HALF), lambda i, j: (i, j)),
            pl.BlockSpec((EDGE_ROWS, FF_BLOCK), lambda i, j: (i, j)),
        ],
        compiler_params=_params(("parallel", "parallel")),
        name="up_conv_gate",
    )(h, w, w, ssq, conv_w, conv_w, conv_b, conv_b)

    tiles_per_seq = s_len // tm
    if tiles_per_seq == 1:
        return act
    conv_w, conv_b = _gate_up_blocks(conv_w), _gate_up_blocks(conv_b)
    inner = tiles_per_seq - 1
    fix_rows = 2 * SUBLANE
    lo_tile = lambda bd: (bd // inner) * tiles_per_seq + bd % inner
    act_blk = lambda bd, side: ((lo_tile(bd) + 1) * (tm // fix_rows) - 1 + side, 0)
    return pl.pallas_call(
        _edge_fix_kernel,
        out_shape=jax.ShapeDtypeStruct((t, f), BF16),
        grid=((t // s_len) * inner, 2),
        in_specs=[
            pl.BlockSpec((fix_rows, f), act_blk),
            pl.BlockSpec((EDGE_ROWS, 2 * f), lambda bd, side: (lo_tile(bd), 0)),
            pl.BlockSpec((EDGE_ROWS, 2 * f), lambda bd, side: (lo_tile(bd) + 1, 0)),
            pl.BlockSpec((3, 2 * f), lambda bd, side: (0, 0)),
            pl.BlockSpec((1, 2 * f), lambda bd, side: (0, 0)),
        ],
        out_specs=pl.BlockSpec((fix_rows, f), act_blk),
        input_output_aliases={0: 0},
        compiler_params=_params(("arbitrary", "arbitrary")),
        name="conv_edge_fix",
    )(act, edges, edges, conv_w, conv_b)


def _rope(x, cos, sin_signed):
    return x * cos + pltpu.roll(x, LANE // 2, 1) * sin_signed


def rope_tables(s_len, d):
    inv = 1.0 / (ROPE_THETA ** (jnp.arange(0, d, 2, dtype=F32) / d))
    ang = jnp.arange(s_len, dtype=F32)[:, None] * inv[None, :]
    cos, sin = jnp.cos(ang), jnp.sin(ang)
    pad = jnp.zeros((s_len, LANE // 2 - d // 2), F32)
    return (jnp.concatenate([cos, pad, cos, pad], axis=-1), jnp.concatenate([-sin, pad, sin, pad], axis=-1))


NA_GROUP = 4
NA_WIN_ROWS = 12
NA_FIRST, NA_INTERIOR, NA_LAST = 0, 1, 2


def _na_kernel(q_ref, k_ref, v_ref, bias_ref, o_ref, *, rows, unroll):
    scale = HEAD_DIM**-0.5
    n_groups = rows // NA_GROUP
    gq = NA_GROUP * GRID_W
    wk = NA_WIN_ROWS * GRID_W

    def one_group(g):
        r0 = g * NA_GROUP
        ws = jnp.clip(r0 - NA_ROWS // 2, 0, rows - NA_WIN_ROWS)
        variant = jnp.where(g == 0, NA_FIRST, jnp.where(g == n_groups - 1, NA_LAST, NA_INTERIOR))
        q0 = pl.multiple_of(r0 * GRID_W, gq)
        k0 = pl.multiple_of(ws * GRID_W, GRID_W)
        s = lax.dot_general(q_ref[pl.ds(q0, gq), :], k_ref[pl.ds(k0, wk), :], _NT, preferred_element_type=F32)
        s = s * (scale * LOG2_E) + bias_ref[variant]
        m = jnp.max(s, axis=-1, keepdims=True)
        p = jnp.exp2(s - m)
        l = jnp.sum(p, axis=-1, keepdims=True)
        o = jnp.dot(p.astype(BF16), v_ref[pl.ds(k0, wk), :], preferred_element_type=F32) / l
        o_ref[pl.ds(q0, gq), :] = o.astype(o_ref.dtype)

    def body(it, carry):
        for u in range(unroll):
            one_group(it * unroll + u)
        return carry

    lax.fori_loop(0, n_groups // unroll, body, 0)


def na_bias_tables(rpb_all):
    qc = np.arange(GRID_W)[:, None]
    kc = np.arange(GRID_W)[None, :]
    q_ws = np.clip(qc - NA_COLS // 2, 0, GRID_W - NA_COLS)
    valid = (kc >= q_ws) & (kc < q_ws + NA_COLS)
    coff = np.clip(kc - qc + NA_COLS - 1, 0, 2 * NA_COLS - 2)
    onehot = (coff[None] == np.arange(2 * NA_COLS - 1)[:, None, None]).astype(np.float32)
    e = jnp.einsum("lhrc,cqk->lhrqk", rpb_all.astype(F32), onehot, precision=lax.Precision.HIGHEST)
    e = jnp.where(valid, e * LOG2_E, NEG_INF)
    pad = NA_WIN_ROWS
    e_pad = jnp.pad(e, ((0, 0), (0, 0), (pad, pad), (0, 0), (0, 0)))
    i = np.arange(NA_WIN_ROWS)
    slabs, valid_rows = [], []
    for r0_minus_ws, first_valid in ((0, lambda u: 0), (NA_ROWS // 2, lambda u: u), (NA_ROWS, lambda u: NA_ROWS // 2)):
        for u in range(NA_GROUP):
            lo = pad + (NA_ROWS - 1) - u - r0_minus_ws
            slabs.append(e_pad[:, :, lo : lo + NA_WIN_ROWS])
            valid_rows.append((i >= first_valid(u)) & (i < first_valid(u) + NA_ROWS))
    tab = jnp.stack(slabs, axis=2)
    valid_rows = np.stack(valid_rows)[:, :, None, None]
    tab = jnp.where(valid_rows, tab, NEG_INF)
    tab = tab.transpose(0, 1, 2, 4, 3, 5)
    return tab.reshape(tab.shape[:2] + (3, NA_GROUP * GRID_W, NA_WIN_ROWS * GRID_W))


def na_attention(proj3, bias_tab):
    b, s_len, _ = proj3.shape
    rows = s_len // GRID_W
    assert rows >= NA_WIN_ROWS and rows % NA_GROUP == 0, "grid too short for the grouped neighbourhood kernel"
    n_groups = rows // NA_GROUP
    blk = lambda off: pl.BlockSpec((None, s_len, LANE), lambda bi, h, off=off: (bi, 0, off + h))
    return pl.pallas_call(
        functools.partial(_na_kernel, rows=rows, unroll=4 if n_groups % 4 == 0 else 1),
        out_shape=jax.ShapeDtypeStruct((b, s_len, A_W), BF16),
        grid=(b, NA_HEADS),
        in_specs=[
            blk(QA_BLK),
            blk(KA_BLK),
            blk(VA_BLK),
            pl.BlockSpec((None, 3, NA_GROUP * GRID_W, NA_WIN_ROWS * GRID_W), lambda bi, h: (h, 0, 0, 0)),
        ],
        out_specs=pl.BlockSpec((None, s_len, LANE), lambda bi, h: (bi, 0, h)),
        compiler_params=_params(("parallel", "parallel")),
        name="na_attention",
    )(proj3, proj3, proj3, bias_tab)


def _band_kernel(*refs, s_len, dil, tq, hw, win, has_sink, want_lse, unroll):
    refs = list(refs)
    q_ref, k_ref, v_ref, cos_ref, sin_ref, mask_ref = refs[:6]
    del refs[:6]
    sink_ref = refs.pop(0) if has_sink else None
    o_ref = refs.pop(0)
    lse_ref = refs.pop(0) if want_lse else None
    stage, qd, kd, vd = refs[:4]
    od = refs[4] if dil > 1 else None
    ld = refs[5] if (dil > 1 and want_lse) else None
    sub = s_len // dil
    nq = sub // tq
    scale = HEAD_DIM**-0.5

    def regroup(val, dst):
        if dil == 1:
            dst[0] = val.astype(BF16)
        else:
            stage[...] = val
            for r in range(dil):
                dst[r] = stage[pl.ds(r, sub, stride=dil), :].astype(BF16)

    cos, sin = cos_ref[...], sin_ref[...]
    regroup(_rope(k_ref[...].astype(F32), cos, sin), kd)
    regroup(_rope(q_ref[...].astype(F32), cos, sin), qd)
    regroup(v_ref[...].astype(F32), vd)

    def one_block(it):
        r = it // nq
        q0 = pl.multiple_of((it % nq) * tq, tq)
        ws = pl.multiple_of(jnp.clip(q0 - hw, 0, sub - win), 64)
        s = lax.dot_general(qd[r, pl.ds(q0, tq), :], kd[r, pl.ds(ws, win), :], _NT, preferred_element_type=F32)
        s = s * (scale * LOG2_E) + mask_ref[(q0 - ws) // hw]
        m = jnp.max(s, axis=-1, keepdims=True)
        if has_sink:
            sk = sink_ref[0:1, 0:1] * LOG2_E
            m = jnp.maximum(m, sk)
        p = jnp.exp2(s - m)
        l = jnp.sum(p, axis=-1, keepdims=True)
        if has_sink:
            l = l + jnp.exp2(sk - m)
        o = jnp.dot(p.astype(BF16), vd[r, pl.ds(ws, win), :], preferred_element_type=F32) / l
        if dil == 1:
            o_ref[pl.ds(q0, tq), :] = o.astype(o_ref.dtype)
        else:
            od[r, pl.ds(q0, tq), :] = o
        if want_lse:
            lse = jnp.broadcast_to(m * LN_2 + jnp.log(l), (tq, LANE))
            if dil == 1:
                lse_ref[pl.ds(q0, tq), :] = lse
            else:
                ld[r, pl.ds(q0, tq), :] = lse

    def body(g, carry):
        for u in range(unroll):
            one_block(g * unroll + u)
        return carry

    lax.fori_loop(0, dil * nq // unroll, body, 0)

    if dil > 1:
        for r in range(dil):
            stage[pl.ds(r, sub, stride=dil), :] = od[r]
        o_ref[...] = stage[...].astype(o_ref.dtype)
        if want_lse:
            for r in range(dil):
                lse_ref[pl.ds(r, sub, stride=dil), :] = ld[r]


def banded_attention(proj3, cos, sin, *, dil, hw, n_heads, q_blk, k_blk, v_blk, kv_group, sink=None, want_lse=False):
    b, s_len, _ = proj3.shape
    sub = s_len // dil
    tq = min(max(128, 2 * hw), sub)
    win = min(tq + 2 * hw, sub)
    total = dil * (sub // tq)
    unroll = 4 if total % 4 == 0 else 1
    out_w = n_heads * LANE

    def col(off, group=1):
        return pl.BlockSpec((None, s_len, LANE), lambda bi, h: (bi, 0, off + h // group))

    offsets = [q0 - min(max(q0 - hw, 0), sub - win) for q0 in range(0, sub, tq)]
    rel = np.arange(win)[None, :] - np.arange(tq)[:, None]
    band = np.stack([np.where(np.abs(rel - off) <= hw, 0.0, NEG_INF) for off in range(0, max(offsets) + 1, hw)])
    band = jnp.asarray(band, F32)

    table = pl.BlockSpec((s_len, LANE), lambda bi, h: (0, 0))
    in_specs = [col(q_blk), col(k_blk, kv_group), col(v_blk, kv_group), table, table]
    in_specs.append(pl.BlockSpec(band.shape, lambda bi, h: (0, 0, 0)))
    args = [proj3, proj3, proj3, cos, sin, band]
    if sink is not None:
        in_specs.append(pl.BlockSpec((None, SUBLANE, LANE), lambda bi, h: (h, 0, 0)))
        args.append(jnp.broadcast_to(sink.astype(F32)[:, None, None], (n_heads, SUBLANE, LANE)))
    out_spec = pl.BlockSpec((None, s_len, LANE), lambda bi, h: (bi, 0, h))
    out_shape = [jax.ShapeDtypeStruct((b, s_len, out_w), BF16)]
    out_specs = [out_spec]
    if want_lse:
        out_shape.append(jax.ShapeDtypeStruct((b, s_len, out_w), F32))
        out_specs.append(out_spec)
    slab = (dil, sub, LANE)
    scratch = [pltpu.VMEM((s_len, LANE), F32), pltpu.VMEM(slab, BF16), pltpu.VMEM(slab, BF16), pltpu.VMEM(slab, BF16)]
    if dil > 1:
        scratch.append(pltpu.VMEM(slab, F32))
        if want_lse:
            scratch.append(pltpu.VMEM(slab, F32))
    return pl.pallas_call(
        functools.partial(
            _band_kernel, s_len=s_len, dil=dil, tq=tq, hw=hw, win=win, has_sink=sink is not None,
            want_lse=want_lse, unroll=unroll,
        ),
        out_shape=out_shape,
        grid=(b, n_heads),
        in_specs=in_specs,
        out_specs=out_specs,
        scratch_shapes=scratch,
        compiler_params=_params(("parallel", "arbitrary")),
        name=f"banded_attention_d{dil}",
    )(*args)


def _merge_kernel(o0_ref, o1_ref, o2_ref, l0_ref, l1_ref, l2_ref, out_ref):
    l0, l1, l2 = l0_ref[...], l1_ref[...], l2_ref[...]
    mx = jnp.maximum(jnp.maximum(l0, l1), l2)
    e0, e1, e2 = jnp.exp(l0 - mx), jnp.exp(l1 - mx), jnp.exp(l2 - mx)
    den = e0 + e1 + e2
    gw = DIL_HPG * LANE
    for g, (o_ref, e) in enumerate(((o0_ref, e0), (o1_ref, e1), (o2_ref, e2))):
        out_ref[:, g * gw : (g + 1) * gw] = (o_ref[...].astype(F32) * (e / den)).astype(out_ref.dtype)


def merge_dilated(outs, lses):
    t, gw = outs[0].shape
    tm = min(512, t)
    spec = pl.BlockSpec((tm, gw), lambda i: (i, 0))
    return pl.pallas_call(
        _merge_kernel,
        out_shape=jax.ShapeDtypeStruct((t, len(outs) * gw), BF16),
        grid=(t // tm,),
        in_specs=[spec] * 6,
        out_specs=pl.BlockSpec((tm, len(outs) * gw), lambda i: (i, 0)),
        compiler_params=_params(("parallel",)),
        name="merge_dilated",
    )(*outs, *lses)


def _mla_pre_kernel(h_ref, ssq_ref, w1_ref, gq_ref, gkv_ref, wuq_ref, wukv_ref, cos_ref, sin_ref, q_ref, k_ref, v_ref):
    c = jnp.dot(h_ref[...], w1_ref[...], preferred_element_type=F32) * _row_scale(ssq_ref, h_ref.shape[1])
    cq = c[:, :MLA_Q_RANK]
    ckv = c[:, MLA_Q_RANK : MLA_Q_RANK + KV_PAD]
    kr = c[:, MLA_Q_RANK + KV_PAD :]
    cos, sin = cos_ref[...], sin_ref[...]

    nq = (cq * lax.rsqrt(jnp.mean(cq * cq, axis=-1, keepdims=True) + NORM_EPS)) * gq_ref[...]
    q = jnp.dot(nq.astype(BF16), wuq_ref[...], preferred_element_type=F32)
    ms = jnp.sum(ckv * ckv, axis=-1, keepdims=True) * (1.0 / MLA_KV_RANK)
    nkv = (ckv * lax.rsqrt(ms + NORM_EPS)) * gkv_ref[...]
    kv = jnp.dot(nkv.astype(BF16), wukv_ref[...], preferred_element_type=F32)
    k_rope = _rope(kr, cos, sin).astype(k_ref.dtype)
    for h in range(MLA_HEADS):
        lo = h * MLA_QK
        q_ref[:, lo : lo + LANE] = q[:, lo : lo + LANE].astype(q_ref.dtype)
        q_ref[:, lo + LANE : lo + MLA_QK] = _rope(q[:, lo + LANE : lo + MLA_QK], cos, sin).astype(q_ref.dtype)
        k_ref[:, lo : lo + LANE] = kv[:, h * LANE : (h + 1) * LANE].astype(k_ref.dtype)
        k_ref[:, lo + LANE : lo + MLA_QK] = k_rope
    v_ref[...] = kv[:, D_OUT_W:].astype(v_ref.dtype)


def mla_pre(h, ssq, w1, gq, gkv, wuq, wukv, cos, sin, s_len):
    t, d = h.shape
    tm = min(256, s_len)
    n_pos = s_len // tm
    const = lambda shape: pl.BlockSpec(shape, lambda i: (0, 0))
    row = lambda w: pl.BlockSpec((tm, w), lambda i: (i, 0))
    return pl.pallas_call(
        _mla_pre_kernel,
        out_shape=[
            jax.ShapeDtypeStruct((t, MLA_HEADS * MLA_QK), BF16),
            jax.ShapeDtypeStruct((t, MLA_HEADS * MLA_QK), BF16),
            jax.ShapeDtypeStruct((t, D_OUT_W), BF16),
        ],
        grid=(t // tm,),
        in_specs=[
            row(d),
            row(LANE),
            const((d, MLA_IN_W)),
            const((1, MLA_Q_RANK)),
            const((1, KV_PAD)),
            const((MLA_Q_RANK, MLA_HEADS * MLA_QK)),
            const((KV_PAD, 2 * D_OUT_W)),
            pl.BlockSpec((tm, LANE), lambda i: (i % n_pos, 0)),
            pl.BlockSpec((tm, LANE), lambda i: (i % n_pos, 0)),
        ],
        out_specs=[row(MLA_HEADS * MLA_QK), row(MLA_HEADS * MLA_QK), row(D_OUT_W)],
        compiler_params=_params(("parallel",)),
        name="mla_pre",
    )(h, ssq, w1, gq, gkv, wuq, wukv, cos, sin)


def _mla_kernel(q_ref, k_ref, v_ref, o_ref, s_a, s_b, *, tq):
    n_blk = q_ref.shape[0] // tq
    c = (MLA_NOPE + MLA_ROPE) ** -0.5 * LOG2_E

    def scores(i, dst):
        q0 = pl.multiple_of(jnp.minimum(i, n_blk - 1) * tq, tq)
        dst[...] = lax.dot_general(q_ref[pl.ds(q0, tq), :], k_ref[...], _NT, preferred_element_type=F32)

    def finish(src, i):
        s = src[...] * c
        m = jnp.max(s, axis=-1, keepdims=True)
        p = jnp.exp2(s - m)
        l = jnp.sum(p, axis=-1, keepdims=True)
        o = jnp.dot(p.astype(BF16), v_ref[...], preferred_element_type=F32) / l
        o_ref[pl.ds(pl.multiple_of(i * tq, tq), tq), :] = o.astype(o_ref.dtype)

    scores(0, s_a)

    def pair(g, carry):
        i = 2 * g
        scores(i + 1, s_b)
        finish(s_a, i)
        scores(i + 2, s_a)
        finish(s_b, i + 1)
        return carry

    lax.fori_loop(0, n_blk // 2, pair, 0)


def mla_attention(q, k, v):
    b, s_len, _ = q.shape
    tq = min(512, s_len // 2)
    blk = lambda w: pl.BlockSpec((None, s_len, w), lambda bi, h: (bi, 0, h))
    return pl.pallas_call(
        functools.partial(_mla_kernel, tq=tq),
        out_shape=jax.ShapeDtypeStruct((b, s_len, D_OUT_W), BF16),
        grid=(b, MLA_HEADS),
        in_specs=[blk(MLA_QK), blk(MLA_QK), blk(LANE)],
        out_specs=blk(LANE),
        scratch_shapes=[pltpu.VMEM((tq, s_len), F32), pltpu.VMEM((tq, s_len), F32)],
        compiler_params=_params(("parallel", "parallel")),
        name="mla_attention",
    )(q, k, v)


def _pad_cols(w, width):
    return jnp.pad(w, ((0, 0),) * (w.ndim - 1) + ((0, width - w.shape[-1]),))


def _interleave_rope(w):
    z = jnp.zeros(w.shape[:-1] + (LANE // 2 - MLA_ROPE // 2,), w.dtype)
    return jnp.concatenate([w[..., : MLA_ROPE // 2], z, w[..., MLA_ROPE // 2 :], z], axis=-1)


def _gate_up_blocks(w):
    f = w.shape[-1] // 2
    lead = w.shape[:-1]
    gate = w[..., :f].reshape(lead + (f // FF_HALF, FF_HALF))
    up = w[..., f:].reshape(lead + (f // FF_HALF, FF_HALF))
    return jnp.stack([gate, up], axis=-2).reshape(lead + (2 * f,))


def prepare_layer(w_in, w_uq, mla_kv_norm, w_ukv, w_out, w_up, conv_w, conv_b, w_down):
    assert w_down.shape[0] % FF_HALF == 0, "d_ff must be a whole number of gate column blocks"
    w_in = w_in.astype(BF16)
    o = MAIN_W
    w_mla = jnp.concatenate(
        [
            w_in[:, o : o + MLA_Q_RANK],
            _pad_cols(w_in[:, o + MLA_Q_RANK : o + MLA_Q_RANK + MLA_KV_RANK], KV_PAD),
            _interleave_rope(w_in[:, o + MLA_Q_RANK + MLA_KV_RANK :]),
        ],
        axis=-1,
    )
    uq = w_uq.reshape(MLA_Q_RANK, MLA_HEADS, MLA_NOPE + MLA_ROPE)
    wuq = jnp.concatenate([uq[..., :MLA_NOPE], _interleave_rope(uq[..., MLA_NOPE:])], axis=-1)
    wuq = wuq.reshape(MLA_Q_RANK, MLA_HEADS * MLA_QK).astype(BF16)
    ukv = w_ukv.reshape(MLA_KV_RANK, MLA_HEADS, MLA_NOPE + MLA_V)
    wukv = jnp.concatenate(
        [ukv[..., :MLA_NOPE].reshape(MLA_KV_RANK, D_OUT_W), ukv[..., MLA_NOPE:].reshape(MLA_KV_RANK, D_OUT_W)], axis=-1
    )
    wukv = jnp.pad(wukv, ((0, KV_PAD - MLA_KV_RANK), (0, 0))).astype(BF16)
    gkv = _pad_cols(mla_kv_norm.reshape(1, MLA_KV_RANK), KV_PAD)
    return dict(
        w_in=w_in,
        w_mla=w_mla,
        wuq=wuq,
        wukv=wukv,
        gkv=gkv,
        w_out=w_out.astype(BF16),
        w_up=w_up.astype(BF16),
        conv_w=conv_w,
        conv_b=conv_b[None],
        w_down=w_down.astype(BF16),
    )


def encoder_layer(x, h, ssq, b, s_len, lw, na_bias, sink, mla_q_norm, ffn_norm, next_attn_norm, tabs):
    t, d = x.shape
    cos128, sin128, cos64, sin64 = tabs
    proj = matmul_scaled(h, lw["w_in"], ssq, BF16, tm=MXU_TILE, tn=MXU_TILE, n_out=MAIN_W)
    proj3 = proj.reshape(b, s_len, proj.shape[1])

    oa = na_attention(proj3, na_bias)

    (ob,) = banded_attention(
        proj3, cos128, sin128, dil=1, hw=SW_WINDOW, n_heads=SW_HEADS, q_blk=QB_BLK, k_blk=KB_BLK, v_blk=VB_BLK,
        kv_group=SW_GROUP, sink=sink,
    )

    outs, lses = [], []
    for gi, (win, dil) in enumerate(DIL_PAIRS):
        o_g, lse_g = banded_attention(
            proj3, cos128, sin128, dil=dil, hw=(win // 2) // dil, n_heads=DIL_HPG, q_blk=QC_BLK + gi * DIL_HPG,
            k_blk=KC_BLK + gi * DIL_HPG, v_blk=VC_BLK + gi * DIL_HPG, kv_group=1, want_lse=True,
        )
        outs.append(o_g.reshape(t, DIL_HPG * LANE))
        lses.append(lse_g.reshape(t, DIL_HPG * LANE))
    oc = merge_dilated(outs, lses)

    q, k, v = mla_pre(
        h, ssq, lw["w_mla"], mla_q_norm.reshape(1, MLA_Q_RANK), lw["gkv"], lw["wuq"], lw["wukv"], cos64, sin64, s_len
    )
    od = mla_attention(q.reshape(b, s_len, -1), k.reshape(b, s_len, -1), v.reshape(b, s_len, -1))

    mix = jnp.concatenate([oa.reshape(t, A_W), ob.reshape(t, B_QW), oc, od.reshape(t, D_OUT_W)], axis=-1)
    x, h2, ssq2 = matmul_residual(mix, lw["w_out"], x, g_next=ffn_norm)

    act = up_conv_gate(h2, ssq2, lw["w_up"], lw["conv_w"], lw["conv_b"], s_len)
    half = MXU_TILE // 2
    if next_attn_norm is None:
        return matmul_residual(act, lw["w_down"], x, tm=half, tn=half), None, None
    return matmul_residual(act, lw["w_down"], x, g_next=next_attn_norm, tm=half, tn=half)


def trunk(x3, layers, na_bias, attn_norm, sink, mla_q_norm, ffn_norm, final_norm):
    b, s_len, d = x3.shape
    tabs = rope_tables(s_len, HEAD_DIM) + rope_tables(s_len, MLA_ROPE)
    x = x3.reshape(b * s_len, d)
    h, ssq = prenorm_rows(x, attn_norm[0])
    for l, lw in enumerate(layers):
        next_attn_norm = attn_norm[l + 1] if l + 1 < len(layers) else None
        x, h, ssq = encoder_layer(
            x, h, ssq, b, s_len, lw, na_bias[l], sink[l], mla_q_norm[l], ffn_norm[l], next_attn_norm, tabs
        )
    return rms_rows(x, final_norm, F32).reshape(b, s_len, d)


def kernel(x_prompt, x_sample, attn_norm, w_in, na_rpb, sink, mla_q_norm, w_uq, mla_kv_norm, w_ukv, w_out, ffn_norm, w_up, conv_w, conv_b, w_down, final_norm):
    depth = w_in.shape[0]
    layers = [
        prepare_layer(w_in[l], w_uq[l], mla_kv_norm[l], w_ukv[l], w_out[l], w_up[l], conv_w[l], conv_b[l], w_down[l])
        for l in range(depth)
    ]
    na_bias = na_bias_tables(na_rpb)
    return tuple(
        trunk(x3, layers, na_bias, attn_norm, sink, mla_q_norm, ffn_norm, final_norm) for x3 in (x_prompt, x_sample)
    )
```

```python
import functools

import numpy as np
import jax
import jax.numpy as jnp
from jax import lax
from jax.experimental import pallas as pl
from jax.experimental.pallas import tpu as pltpu

F32 = jnp.float32
BF16 = jnp.bfloat16

HEAD_DIM = 128
ROPE_THETA = 10000.0
NORM_EPS = 1e-6
NEG_INF = -1e30
GRID_W = 64
NA_HEADS = 8
NA_ROWS = 8
NA_COLS = 16
SW_HEADS = 8
SW_KV_HEADS = 2
SW_GROUP = SW_HEADS // SW_KV_HEADS
SW_WINDOW = 128
DIL_PAIRS = ((128, 1), (512, 4), (2048, 16))
DIL_HPG = 3
DIL_HEADS = DIL_HPG * len(DIL_PAIRS)
MLA_HEADS = 7
MLA_Q_RANK = 896
MLA_KV_RANK = 320
MLA_NOPE = 128
MLA_ROPE = 64
MLA_V = 128

A_W = NA_HEADS * HEAD_DIM
B_QW = SW_HEADS * HEAD_DIM
B_KW = SW_KV_HEADS * HEAD_DIM
C_W = DIL_HEADS * HEAD_DIM
D_OUT_W = MLA_HEADS * MLA_V
MAIN_W = 3 * A_W + B_QW + 2 * B_KW + 3 * C_W

LANE = 128
SUBLANE = 8
MXU_TILE = 1024
VMEM_LIMIT = 56 * 1024 * 1024

QA_BLK, KA_BLK, VA_BLK = 0, A_W // LANE, 2 * A_W // LANE
QB_BLK = 3 * A_W // LANE
KB_BLK = QB_BLK + B_QW // LANE
VB_BLK = KB_BLK + B_KW // LANE
QC_BLK = VB_BLK + B_KW // LANE
KC_BLK = QC_BLK + C_W // LANE
VC_BLK = KC_BLK + C_W // LANE

KV_PAD = 384
MLA_IN_W = MLA_Q_RANK + KV_PAD + LANE
MLA_QK = 2 * LANE

_NT = (((1,), (1,)), ((), ()))
LOG2_E = 1.4426950408889634
LN_2 = 0.6931471805599453
MLA_Q_SCALE = (MLA_NOPE + MLA_ROPE) ** -0.5 * LOG2_E


def _round_up(x, m):
    return (x + m - 1) // m * m


def _params(sem, vmem=VMEM_LIMIT):
    return pltpu.CompilerParams(dimension_semantics=sem, vmem_limit_bytes=vmem)


def _rms_kernel(x_ref, g_ref, o_ref):
    x = x_ref[...]
    ms = jnp.mean(x * x, axis=-1, keepdims=True)
    o_ref[...] = ((x * lax.rsqrt(ms + NORM_EPS)) * g_ref[...]).astype(o_ref.dtype)


def rms_rows(x, g, out_dtype):
    t, d = x.shape
    tm = min(256, t)
    return pl.pallas_call(
        _rms_kernel,
        out_shape=jax.ShapeDtypeStruct((t, d), out_dtype),
        grid=(t // tm,),
        in_specs=[pl.BlockSpec((tm, d), lambda i: (i, 0)), pl.BlockSpec((1, d), lambda i: (0, 0))],
        out_specs=pl.BlockSpec((tm, d), lambda i: (i, 0)),
        compiler_params=_params(("parallel",)),
        name="rms_rows",
    )(x, g.reshape(1, d))


def _row_scale(ssq_ref, d):
    return lax.rsqrt(jnp.sum(ssq_ref[...], axis=-1, keepdims=True) * (1.0 / d) + NORM_EPS)


def _lane_partial_sumsq(x):
    xx = x * x
    part = xx[:, :LANE]
    for c in range(1, x.shape[1] // LANE):
        part = part + xx[:, c * LANE : (c + 1) * LANE]
    return part


def _prenorm_kernel(x_ref, g_ref, h_ref, ssq_ref):
    x = x_ref[...]
    h_ref[...] = (x * g_ref[...]).astype(h_ref.dtype)
    ssq_ref[...] = _lane_partial_sumsq(x)


def prenorm_rows(x, g):
    t, d = x.shape
    tm = min(256, t)
    return pl.pallas_call(
        _prenorm_kernel,
        out_shape=[jax.ShapeDtypeStruct((t, d), BF16), jax.ShapeDtypeStruct((t, LANE), F32)],
        grid=(t // tm,),
        in_specs=[pl.BlockSpec((tm, d), lambda i: (i, 0)), pl.BlockSpec((1, d), lambda i: (0, 0))],
        out_specs=[pl.BlockSpec((tm, d), lambda i: (i, 0)), pl.BlockSpec((tm, LANE), lambda i: (i, 0))],
        compiler_params=_params(("parallel",)),
        name="prenorm_rows",
    )(x, g.reshape(1, d))


def _largest_tile(n, candidates):
    return next(c for c in candidates if n % c == 0)


def _weight_spec(layer, k, tn, col=lambda j: j):
    return pl.BlockSpec((None, k, tn), lambda i, j: (layer, 0, col(j)))


def _mm_scaled_kernel(a_ref, b_ref, ssq_ref, o_ref, *, d):
    acc = jnp.dot(a_ref[...], b_ref[...], preferred_element_type=F32)
    o_ref[...] = (acc * _row_scale(ssq_ref, d)).astype(o_ref.dtype)


def matmul_scaled(a, b, layer, ssq, out_dtype, tm, tn, n_out=None):
    m, k = a.shape
    tm = min(tm, m)
    if n_out is None:
        n = b.shape[2]
        tn = _largest_tile(n, (tn, 512, 256, LANE))
    else:
        n = _round_up(n_out, tn)
        assert n <= b.shape[2]
    return pl.pallas_call(
        functools.partial(_mm_scaled_kernel, d=k),
        out_shape=jax.ShapeDtypeStruct((m, n), out_dtype),
        grid=(m // tm, n // tn),
        in_specs=[
            pl.BlockSpec((tm, k), lambda i, j: (i, 0)),
            _weight_spec(layer, k, tn),
            pl.BlockSpec((tm, LANE), lambda i, j: (i, 0)),
        ],
        out_specs=pl.BlockSpec((tm, tn), lambda i, j: (i, j)),
        compiler_params=_params(("parallel", "parallel")),
        name="matmul_scaled",
    )(a, b, ssq)


def _mm_res_kernel(a_ref, b_ref, r_ref, o_ref):
    o_ref[...] = r_ref[...] + jnp.dot(a_ref[...], b_ref[...], preferred_element_type=F32)


def _mm_res_prenorm_kernel(a_ref, b_ref, r_ref, g_ref, o_ref, h_ref, ssq_ref):
    x = r_ref[...] + jnp.dot(a_ref[...], b_ref[...], preferred_element_type=F32)
    o_ref[...] = x
    h_ref[...] = (x * g_ref[...]).astype(h_ref.dtype)

    @pl.when(pl.program_id(1) == 0)
    def _():
        ssq_ref[...] = jnp.zeros_like(ssq_ref)

    ssq_ref[...] += _lane_partial_sumsq(x)


def matmul_residual(a, b, layer, res, g_next=None, tm=MXU_TILE, tn=MXU_TILE // 2):
    m, k = a.shape
    n = b.shape[2]
    tm, tn = min(tm, m), min(tn, n)
    in_specs = [
        pl.BlockSpec((tm, k), lambda i, j: (i, 0)),
        _weight_spec(layer, k, tn),
        pl.BlockSpec((tm, tn), lambda i, j: (i, j)),
    ]
    tile = pl.BlockSpec((tm, tn), lambda i, j: (i, j))
    if g_next is None:
        return pl.pallas_call(
            _mm_res_kernel,
            out_shape=jax.ShapeDtypeStruct((m, n), F32),
            grid=(m // tm, n // tn),
            in_specs=in_specs,
            out_specs=tile,
            compiler_params=_params(("parallel", "parallel")),
            name="matmul_residual",
        )(a, b, res)
    return pl.pallas_call(
        _mm_res_prenorm_kernel,
        out_shape=[
            jax.ShapeDtypeStruct((m, n), F32),
            jax.ShapeDtypeStruct((m, n), BF16),
            jax.ShapeDtypeStruct((m, LANE), F32),
        ],
        grid=(m // tm, n // tn),
        in_specs=in_specs + [pl.BlockSpec((1, tn), lambda i, j: (0, j))],
        out_specs=[tile, tile, pl.BlockSpec((tm, LANE), lambda i, j: (i, 0))],
        compiler_params=_params(("parallel", "arbitrary")),
        name="matmul_residual_prenorm",
    )(a, b, res, g_next.reshape(1, n))


FF_BLOCK = 512
FF_HALF = FF_BLOCK // 2
EDGE_ROWS = 2 * SUBLANE


def _conv3_gate(prev, cur, nxt, w, b):
    y = prev * w[0:1] + cur * w[1:2] + nxt * w[2:3] + b
    gate, up = y[:, :FF_HALF], y[:, FF_HALF:]
    return (gate / (1.0 + jnp.exp2(gate * (-LOG2_E)))) * up


def _up_conv_gate_kernel(h_ref, wg_ref, wu_ref, ssq_ref, cwg_ref, cwu_ref, cbg_ref, cbu_ref, act_ref, edge_ref, *, d):
    tm = h_ref.shape[0]
    scale = _row_scale(ssq_ref, d)
    row = lax.broadcasted_iota(jnp.int32, (tm, FF_HALF), 0)

    def conv_branch(w_ref, cw_ref, cb_ref, lane0):
        u = jnp.dot(h_ref[...], w_ref[...], preferred_element_type=F32) * scale
        edge_ref[0:SUBLANE, lane0 : lane0 + FF_HALF] = u[0:SUBLANE]
        edge_ref[SUBLANE:, lane0 : lane0 + FF_HALF] = u[tm - SUBLANE : tm]
        prev = jnp.where(row == 0, 0.0, pltpu.roll(u, 1, 0))
        nxt = jnp.where(row == tm - 1, 0.0, pltpu.roll(u, tm - 1, 0))
        w = cw_ref[...]
        return prev * w[0:1] + u * w[1:2] + nxt * w[2:3] + cb_ref[...]

    gate = conv_branch(wg_ref, cwg_ref, cbg_ref, 0)
    up = conv_branch(wu_ref, cwu_ref, cbu_ref, FF_HALF)
    act_ref[...] = ((gate / (1.0 + jnp.exp2(gate * (-LOG2_E)))) * up).astype(act_ref.dtype)


def _edge_fix_kernel(act_ref, lo_ref, hi_ref, cw_ref, cb_ref, o_ref):
    side = pl.program_id(1)
    rows = act_ref.shape[0]
    last = side == 0
    e = EDGE_ROWS
    prev = jnp.where(last, lo_ref[e - 2 : e - 1, :], lo_ref[e - 1 : e, :])
    cur = jnp.where(last, lo_ref[e - 1 : e, :], hi_ref[0:1, :])
    nxt = jnp.where(last, hi_ref[0:1, :], hi_ref[1:2, :])
    w, b = cw_ref[...], cb_ref[...]
    pieces = []
    for j in range(w.shape[1] // FF_BLOCK):
        sl = slice(j * FF_BLOCK, (j + 1) * FF_BLOCK)
        pieces.append(_conv3_gate(prev[:, sl], cur[:, sl], nxt[:, sl], w[:, sl], b[:, sl]))
    new_row = jnp.concatenate(pieces, axis=-1)
    row = lax.broadcasted_iota(jnp.int32, act_ref.shape, 0)
    target = jnp.where(last, rows - 1, 0)
    o_ref[...] = jnp.where(row == target, new_row, act_ref[...].astype(F32)).astype(o_ref.dtype)


def up_conv_gate(h, ssq, w, layer, conv_w, conv_b, s_len):
    t, d = h.shape
    f = w.shape[2] // 2
    tm = min(2 * MXU_TILE, s_len)
    n_i, n_j = t // tm, f // FF_HALF
    act, edges = pl.pallas_call(
        functools.partial(_up_conv_gate_kernel, d=d),
        out_shape=[jax.ShapeDtypeStruct((t, f), BF16), jax.ShapeDtypeStruct((n_i * EDGE_ROWS, 2 * f), F32)],
        grid=(n_i, n_j),
        in_specs=[
            pl.BlockSpec((tm, d), lambda i, j: (i, 0)),
            _weight_spec(layer, d, FF_HALF),
            _weight_spec(layer, d, FF_HALF, lambda j: j + n_j),
            pl.BlockSpec((tm, LANE), lambda i, j: (i, 0)),
            pl.BlockSpec((3, FF_HALF), lambda i, j: (0, j)),
            pl.BlockSpec((3, FF_HALF), lambda i, j: (0, j + n_j)),
            pl.BlockSpec((1, FF_HALF), lambda i, j: (0, j)),
            pl.BlockSpec((1, FF_HALF), lambda i, j: (0, j + n_j)),
        ],
        out_specs=[
            pl.BlockSpec((tm, FF_HALF), lambda i, j: (i, j)),
            pl.BlockSpec((EDGE_ROWS, FF_BLOCK), lambda i, j: (i, j)),
        ],
        compiler_params=_params(("parallel", "parallel")),
        name="up_conv_gate",
    )(h, w, w, ssq, conv_w, conv_w, conv_b, conv_b)

    tiles_per_seq = s_len // tm
    if tiles_per_seq == 1:
        return act
    conv_w, conv_b = _gate_up_blocks(conv_w), _gate_up_blocks(conv_b)
    inner = tiles_per_seq - 1
    fix_rows = 2 * SUBLANE
    lo_tile = lambda bd: (bd // inner) * tiles_per_seq + bd % inner
    act_blk = lambda bd, side: ((lo_tile(bd) + 1) * (tm // fix_rows) - 1 + side, 0)
    return pl.pallas_call(
        _edge_fix_kernel,
        out_shape=jax.ShapeDtypeStruct((t, f), BF16),
        grid=((t // s_len) * inner, 2),
        in_specs=[
            pl.BlockSpec((fix_rows, f), act_blk),
            pl.BlockSpec((EDGE_ROWS, 2 * f), lambda bd, side: (lo_tile(bd), 0)),
            pl.BlockSpec((EDGE_ROWS, 2 * f), lambda bd, side: (lo_tile(bd) + 1, 0)),
            pl.BlockSpec((3, 2 * f), lambda bd, side: (0, 0)),
            pl.BlockSpec((1, 2 * f), lambda bd, side: (0, 0)),
        ],
        out_specs=pl.BlockSpec((fix_rows, f), act_blk),
        input_output_aliases={0: 0},
        compiler_params=_params(("arbitrary", "arbitrary")),
        name="conv_edge_fix",
    )(act, edges, edges, conv_w, conv_b)


def _rope(x, cos, sin_signed):
    return x * cos + pltpu.roll(x, LANE // 2, 1) * sin_signed


def rope_tables(s_len, d):
    inv = 1.0 / (ROPE_THETA ** (jnp.arange(0, d, 2, dtype=F32) / d))
    ang = jnp.arange(s_len, dtype=F32)[:, None] * inv[None, :]
    cos, sin = jnp.cos(ang), jnp.sin(ang)
    pad = jnp.zeros((s_len, LANE // 2 - d // 2), F32)
    return (jnp.concatenate([cos, pad, cos, pad], axis=-1), jnp.concatenate([-sin, pad, sin, pad], axis=-1))


NA_GROUP = 4
NA_WIN_ROWS = 12
NA_FIRST, NA_INTERIOR, NA_LAST = 0, 1, 2


def _na_kernel(q_ref, k_ref, v_ref, bias_ref, o_ref, *, rows, unroll):
    scale = HEAD_DIM**-0.5
    n_groups = rows // NA_GROUP
    gq = NA_GROUP * GRID_W
    wk = NA_WIN_ROWS * GRID_W

    def one_group(g):
        r0 = g * NA_GROUP
        ws = jnp.clip(r0 - NA_ROWS // 2, 0, rows - NA_WIN_ROWS)
        variant = jnp.where(g == 0, NA_FIRST, jnp.where(g == n_groups - 1, NA_LAST, NA_INTERIOR))
        q0 = pl.multiple_of(r0 * GRID_W, gq)
        k0 = pl.multiple_of(ws * GRID_W, GRID_W)
        s = lax.dot_general(q_ref[pl.ds(q0, gq), :], k_ref[pl.ds(k0, wk), :], _NT, preferred_element_type=F32)
        s = s * (scale * LOG2_E) + bias_ref[variant]
        m = jnp.max(s, axis=-1, keepdims=True)
        p = jnp.exp2(s - m)
        l = jnp.sum(p, axis=-1, keepdims=True)
        o = jnp.dot(p.astype(BF16), v_ref[pl.ds(k0, wk), :], preferred_element_type=F32) / l
        o_ref[pl.ds(q0, gq), :] = o.astype(o_ref.dtype)

    def body(it, carry):
        for u in range(unroll):
            one_group(it * unroll + u)
        return carry

    lax.fori_loop(0, n_groups // unroll, body, 0)


def na_bias_tables(rpb_all):
    qc = np.arange(GRID_W)[:, None]
    kc = np.arange(GRID_W)[None, :]
    q_ws = np.clip(qc - NA_COLS // 2, 0, GRID_W - NA_COLS)
    valid = (kc >= q_ws) & (kc < q_ws + NA_COLS)
    coff = np.clip(kc - qc + NA_COLS - 1, 0, 2 * NA_COLS - 2)
    onehot = (coff[None] == np.arange(2 * NA_COLS - 1)[:, None, None]).astype(np.float32)
    e = jnp.einsum("lhrc,cqk->lhrqk", rpb_all.astype(F32), onehot, precision=lax.Precision.HIGHEST)
    e = jnp.where(valid, e * LOG2_E, NEG_INF)
    pad = NA_WIN_ROWS
    e_pad = jnp.pad(e, ((0, 0), (0, 0), (pad, pad), (0, 0), (0, 0)))
    i = np.arange(NA_WIN_ROWS)
    slabs, valid_rows = [], []
    for r0_minus_ws, first_valid in ((0, lambda u: 0), (NA_ROWS // 2, lambda u: u), (NA_ROWS, lambda u: NA_ROWS // 2)):
        for u in range(NA_GROUP):
            lo = pad + (NA_ROWS - 1) - u - r0_minus_ws
            slabs.append(e_pad[:, :, lo : lo + NA_WIN_ROWS])
            valid_rows.append((i >= first_valid(u)) & (i < first_valid(u) + NA_ROWS))
    tab = jnp.stack(slabs, axis=2)
    valid_rows = np.stack(valid_rows)[:, :, None, None]
    tab = jnp.where(valid_rows, tab, NEG_INF)
    tab = tab.transpose(0, 1, 2, 4, 3, 5)
    return tab.reshape(tab.shape[:2] + (3, NA_GROUP * GRID_W, NA_WIN_ROWS * GRID_W))


def na_attention(proj3, bias_tab):
    b, s_len, _ = proj3.shape
    rows = s_len // GRID_W
    assert rows >= NA_WIN_ROWS and rows % NA_GROUP == 0, "grid too short for the grouped neighbourhood kernel"
    n_groups = rows // NA_GROUP
    blk = lambda off: pl.BlockSpec((None, s_len, LANE), lambda bi, h, off=off: (bi, 0, off + h))
    return pl.pallas_call(
        functools.partial(_na_kernel, rows=rows, unroll=4 if n_groups % 4 == 0 else 1),
        out_shape=jax.ShapeDtypeStruct((b, s_len, A_W), BF16),
        grid=(b, NA_HEADS),
        in_specs=[
            blk(QA_BLK),
            blk(KA_BLK),
            blk(VA_BLK),
            pl.BlockSpec((None, 3, NA_GROUP * GRID_W, NA_WIN_ROWS * GRID_W), lambda bi, h: (h, 0, 0, 0)),
        ],
        out_specs=pl.BlockSpec((None, s_len, LANE), lambda bi, h: (bi, 0, h)),
        compiler_params=_params(("parallel", "parallel")),
        name="na_attention",
    )(proj3, proj3, proj3, bias_tab)


def _band_kernel(*refs, s_len, dil, tq, hw, win, has_sink, want_lse, unroll):
    refs = list(refs)
    q_ref, k_ref, v_ref, cos_ref, sin_ref, mask_ref = refs[:6]
    del refs[:6]
    sink_ref = refs.pop(0) if has_sink else None
    o_ref = refs.pop(0)
    lse_ref = refs.pop(0) if want_lse else None
    stage, qd, kd, vd = refs[:4]
    od = refs[4] if dil > 1 else None
    ld = refs[5] if (dil > 1 and want_lse) else None
    sub = s_len // dil
    nq = sub // tq
    scale = HEAD_DIM**-0.5

    def regroup(val, dst):
        if dil == 1:
            dst[0] = val.astype(BF16)
        else:
            stage[...] = val
            for r in range(dil):
                dst[r] = stage[pl.ds(r, sub, stride=dil), :].astype(BF16)

    cos, sin = cos_ref[...], sin_ref[...]
    regroup(_rope(k_ref[...].astype(F32), cos, sin), kd)
    regroup(_rope(q_ref[...].astype(F32), cos, sin), qd)
    regroup(v_ref[...].astype(F32), vd)

    def one_block(it):
        r = it // nq
        q0 = pl.multiple_of((it % nq) * tq, tq)
        ws = pl.multiple_of(jnp.clip(q0 - hw, 0, sub - win), 64)
        s = lax.dot_general(qd[r, pl.ds(q0, tq), :], kd[r, pl.ds(ws, win), :], _NT, preferred_element_type=F32)
        s = s * (scale * LOG2_E) + mask_ref[(q0 - ws) // hw]
        m = jnp.max(s, axis=-1, keepdims=True)
        if has_sink:
            sk = sink_ref[0:1, 0:1] * LOG2_E
            m = jnp.maximum(m, sk)
        p = jnp.exp2(s - m)
        l = jnp.sum(p, axis=-1, keepdims=True)
        if has_sink:
            l = l + jnp.exp2(sk - m)
        o = jnp.dot(p.astype(BF16), vd[r, pl.ds(ws, win), :], preferred_element_type=F32) / l
        if dil == 1:
            o_ref[pl.ds(q0, tq), :] = o.astype(o_ref.dtype)
        else:
            od[r, pl.ds(q0, tq), :] = o
        if want_lse:
            lse = jnp.broadcast_to(m * LN_2 + jnp.log(l), (tq, LANE))
            if dil == 1:
                lse_ref[pl.ds(q0, tq), :] = lse
            else:
                ld[r, pl.ds(q0, tq), :] = lse

    def body(g, carry):
        for u in range(unroll):
            one_block(g * unroll + u)
        return carry

    lax.fori_loop(0, dil * nq // unroll, body, 0)

    if dil > 1:
        for r in range(dil):
            stage[pl.ds(r, sub, stride=dil), :] = od[r]
        o_ref[...] = stage[...].astype(o_ref.dtype)
        if want_lse:
            for r in range(dil):
                lse_ref[pl.ds(r, sub, stride=dil), :] = ld[r]


def banded_attention(proj3, cos, sin, *, dil, hw, n_heads, q_blk, k_blk, v_blk, kv_group, sink=None, want_lse=False):
    b, s_len, _ = proj3.shape
    sub = s_len // dil
    tq = min(max(128, 2 * hw), sub)
    win = min(tq + 2 * hw, sub)
    total = dil * (sub // tq)
    unroll = 4 if total % 4 == 0 else 1
    out_w = n_heads * LANE

    def col(off, group=1):
        return pl.BlockSpec((None, s_len, LANE), lambda bi, h: (bi, 0, off + h // group))

    offsets = [q0 - min(max(q0 - hw, 0), sub - win) for q0 in range(0, sub, tq)]
    rel = np.arange(win)[None, :] - np.arange(tq)[:, None]
    band = np.stack([np.where(np.abs(rel - off) <= hw, 0.0, NEG_INF) for off in range(0, max(offsets) + 1, hw)])
    band = jnp.asarray(band, F32)

    table = pl.BlockSpec((s_len, LANE), lambda bi, h: (0, 0))
    in_specs = [col(q_blk), col(k_blk, kv_group), col(v_blk, kv_group), table, table]
    in_specs.append(pl.BlockSpec(band.shape, lambda bi, h: (0, 0, 0)))
    args = [proj3, proj3, proj3, cos, sin, band]
    if sink is not None:
        in_specs.append(pl.BlockSpec((None, SUBLANE, LANE), lambda bi, h: (h, 0, 0)))
        args.append(jnp.broadcast_to(sink.astype(F32)[:, None, None], (n_heads, SUBLANE, LANE)))
    out_spec = pl.BlockSpec((None, s_len, LANE), lambda bi, h: (bi, 0, h))
    out_shape = [jax.ShapeDtypeStruct((b, s_len, out_w), BF16)]
    out_specs = [out_spec]
    if want_lse:
        out_shape.append(jax.ShapeDtypeStruct((b, s_len, out_w), F32))
        out_specs.append(out_spec)
    slab = (dil, sub, LANE)
    scratch = [pltpu.VMEM((s_len, LANE), F32), pltpu.VMEM(slab, BF16), pltpu.VMEM(slab, BF16), pltpu.VMEM(slab, BF16)]
    if dil > 1:
        scratch.append(pltpu.VMEM(slab, F32))
        if want_lse:
            scratch.append(pltpu.VMEM(slab, F32))
    return pl.pallas_call(
        functools.partial(
            _band_kernel, s_len=s_len, dil=dil, tq=tq, hw=hw, win=win, has_sink=sink is not None,
            want_lse=want_lse, unroll=unroll,
        ),
        out_shape=out_shape,
        grid=(b, n_heads),
        in_specs=in_specs,
        out_specs=out_specs,
        scratch_shapes=scratch,
        compiler_params=_params(("parallel", "arbitrary")),
        name=f"banded_attention_d{dil}",
    )(*args)


def _merge_kernel(o0_ref, o1_ref, o2_ref, l0_ref, l1_ref, l2_ref, out_ref):
    l0, l1, l2 = l0_ref[...], l1_ref[...], l2_ref[...]
    mx = jnp.maximum(jnp.maximum(l0, l1), l2)
    e0, e1, e2 = jnp.exp(l0 - mx), jnp.exp(l1 - mx), jnp.exp(l2 - mx)
    den = e0 + e1 + e2
    gw = DIL_HPG * LANE
    for g, (o_ref, e) in enumerate(((o0_ref, e0), (o1_ref, e1), (o2_ref, e2))):
        out_ref[:, g * gw : (g + 1) * gw] = (o_ref[...].astype(F32) * (e / den)).astype(out_ref.dtype)


def merge_dilated(outs, lses):
    t, gw = outs[0].shape
    tm = min(512, t)
    spec = pl.BlockSpec((tm, gw), lambda i: (i, 0))
    return pl.pallas_call(
        _merge_kernel,
        out_shape=jax.ShapeDtypeStruct((t, len(outs) * gw), BF16),
        grid=(t // tm,),
        in_specs=[spec] * 6,
        out_specs=pl.BlockSpec((tm, len(outs) * gw), lambda i: (i, 0)),
        compiler_params=_params(("parallel",)),
        name="merge_dilated",
    )(*outs, *lses)


def _mla_pre_kernel(h_ref, ssq_ref, w1_ref, gq_ref, gkv_ref, wuq_ref, wukv_ref, cos_ref, sin_ref, q_ref, k_ref, v_ref):
    c = jnp.dot(h_ref[...], w1_ref[...], preferred_element_type=F32) * _row_scale(ssq_ref, h_ref.shape[1])
    cq = c[:, :MLA_Q_RANK]
    ckv = c[:, MLA_Q_RANK : MLA_Q_RANK + KV_PAD]
    kr = c[:, MLA_Q_RANK + KV_PAD :]
    cos, sin = cos_ref[...], sin_ref[...]

    nq = (cq * lax.rsqrt(jnp.mean(cq * cq, axis=-1, keepdims=True) + NORM_EPS)) * gq_ref[...]
    q = jnp.dot(nq.astype(BF16), wuq_ref[...], preferred_element_type=F32) * MLA_Q_SCALE
    ms = jnp.sum(ckv * ckv, axis=-1, keepdims=True) * (1.0 / MLA_KV_RANK)
    nkv = (ckv * lax.rsqrt(ms + NORM_EPS)) * gkv_ref[...]
    kv = jnp.dot(nkv.astype(BF16), wukv_ref[...], preferred_element_type=F32)
    k_rope = _rope(kr, cos, sin).astype(k_ref.dtype)
    for h in range(MLA_HEADS):
        lo = h * MLA_QK
        q_ref[:, lo : lo + LANE] = q[:, lo : lo + LANE].astype(q_ref.dtype)
        q_ref[:, lo + LANE : lo + MLA_QK] = _rope(q[:, lo + LANE : lo + MLA_QK], cos, sin).astype(q_ref.dtype)
        k_ref[:, lo : lo + LANE] = kv[:, h * LANE : (h + 1) * LANE].astype(k_ref.dtype)
        k_ref[:, lo + LANE : lo + MLA_QK] = k_rope
        v_ref[:, lo : lo + LANE] = kv[:, D_OUT_W + h * LANE : D_OUT_W + (h + 1) * LANE].astype(v_ref.dtype)
        v_ref[:, lo + LANE : lo + MLA_QK] = jnp.ones((v_ref.shape[0], LANE), v_ref.dtype)


def mla_pre(h, ssq, w1, gq, gkv, wuq, wukv, cos, sin, s_len):
    t, d = h.shape
    tm = min(256, s_len)
    n_pos = s_len // tm
    const = lambda shape: pl.BlockSpec(shape, lambda i: (0, 0))
    row = lambda w: pl.BlockSpec((tm, w), lambda i: (i, 0))
    return pl.pallas_call(
        _mla_pre_kernel,
        out_shape=[
            jax.ShapeDtypeStruct((t, MLA_HEADS * MLA_QK), BF16),
            jax.ShapeDtypeStruct((t, MLA_HEADS * MLA_QK), BF16),
            jax.ShapeDtypeStruct((t, MLA_HEADS * MLA_QK), BF16),
        ],
        grid=(t // tm,),
        in_specs=[
            row(d),
            row(LANE),
            const((d, MLA_IN_W)),
            const((1, MLA_Q_RANK)),
            const((1, KV_PAD)),
            const((MLA_Q_RANK, MLA_HEADS * MLA_QK)),
            const((KV_PAD, 2 * D_OUT_W)),
            pl.BlockSpec((tm, LANE), lambda i: (i % n_pos, 0)),
            pl.BlockSpec((tm, LANE), lambda i: (i % n_pos, 0)),
        ],
        out_specs=[row(MLA_HEADS * MLA_QK), row(MLA_HEADS * MLA_QK), row(MLA_HEADS * MLA_QK)],
        compiler_params=_params(("parallel",)),
        name="mla_pre",
    )(h, ssq, w1, gq, gkv, wuq, wukv, cos, sin)


def _mla_kernel(q_ref, k_ref, v_ref, o_ref, s_a, s_b, *, tq):
    n_blk = q_ref.shape[0] // tq

    def scores(i, dst):
        q0 = pl.multiple_of(jnp.minimum(i, n_blk - 1) * tq, tq)
        dst[...] = lax.dot_general(q_ref[pl.ds(q0, tq), :], k_ref[...], _NT, preferred_element_type=F32)

    def finish(src, i):
        s = src[...]
        p = jnp.exp2(s - jnp.max(s, axis=-1, keepdims=True))
        ov = jnp.dot(p.astype(BF16), v_ref[...], preferred_element_type=F32)
        o = ov[:, :LANE] / ov[:, LANE : LANE + 1]
        o_ref[pl.ds(pl.multiple_of(i * tq, tq), tq), :] = o.astype(o_ref.dtype)

    scores(0, s_a)

    def pair(g, carry):
        i = 2 * g
        scores(i + 1, s_b)
        finish(s_a, i)
        scores(i + 2, s_a)
        finish(s_b, i + 1)
        return carry

    lax.fori_loop(0, n_blk // 2, pair, 0)


def mla_attention(q, k, v):
    b, s_len, _ = q.shape
    tq = min(512, s_len // 2)
    blk = lambda w: pl.BlockSpec((None, s_len, w), lambda bi, h: (bi, 0, h))
    return pl.pallas_call(
        functools.partial(_mla_kernel, tq=tq),
        out_shape=jax.ShapeDtypeStruct((b, s_len, D_OUT_W), BF16),
        grid=(b, MLA_HEADS),
        in_specs=[blk(MLA_QK), blk(MLA_QK), blk(MLA_QK)],
        out_specs=blk(LANE),
        scratch_shapes=[pltpu.VMEM((tq, s_len), F32), pltpu.VMEM((tq, s_len), F32)],
        compiler_params=_params(("parallel", "parallel")),
        name="mla_attention",
    )(q, k, v)


def _pad_cols(w, width):
    return jnp.pad(w, ((0, 0),) * (w.ndim - 1) + ((0, width - w.shape[-1]),))


def _interleave_rope(w):
    z = jnp.zeros(w.shape[:-1] + (LANE // 2 - MLA_ROPE // 2,), w.dtype)
    return jnp.concatenate([w[..., : MLA_ROPE // 2], z, w[..., MLA_ROPE // 2 :], z], axis=-1)


def _gate_up_blocks(w):
    f = w.shape[-1] // 2
    lead = w.shape[:-1]
    gate = w[..., :f].reshape(lead + (f // FF_HALF, FF_HALF))
    up = w[..., f:].reshape(lead + (f // FF_HALF, FF_HALF))
    return jnp.stack([gate, up], axis=-2).reshape(lead + (2 * f,))


def prepare_layer(w_in_mla, w_uq, mla_kv_norm, w_ukv, conv_w, conv_b):
    w_mla = jnp.concatenate(
        [
            w_in_mla[:, :MLA_Q_RANK],
            _pad_cols(w_in_mla[:, MLA_Q_RANK : MLA_Q_RANK + MLA_KV_RANK], KV_PAD),
            _interleave_rope(w_in_mla[:, MLA_Q_RANK + MLA_KV_RANK :]),
        ],
        axis=-1,
    )
    uq = w_uq.reshape(MLA_Q_RANK, MLA_HEADS, MLA_NOPE + MLA_ROPE)
    wuq = jnp.concatenate([uq[..., :MLA_NOPE], _interleave_rope(uq[..., MLA_NOPE:])], axis=-1)
    wuq = wuq.reshape(MLA_Q_RANK, MLA_HEADS * MLA_QK).astype(BF16)
    ukv = w_ukv.reshape(MLA_KV_RANK, MLA_HEADS, MLA_NOPE + MLA_V)
    wukv = jnp.concatenate(
        [ukv[..., :MLA_NOPE].reshape(MLA_KV_RANK, D_OUT_W), ukv[..., MLA_NOPE:].reshape(MLA_KV_RANK, D_OUT_W)], axis=-1
    )
    wukv = jnp.pad(wukv, ((0, KV_PAD - MLA_KV_RANK), (0, 0))).astype(BF16)
    gkv = _pad_cols(mla_kv_norm.reshape(1, MLA_KV_RANK), KV_PAD)
    return dict(w_mla=w_mla, wuq=wuq, wukv=wukv, gkv=gkv, conv_w=conv_w, conv_b=conv_b[None])


def encoder_layer(x, h, ssq, b, s_len, big, l, lw, na_bias, sink, mla_q_norm, ffn_norm, next_attn_norm, tabs):
    t, d = x.shape
    cos128, sin128, cos64, sin64 = tabs
    proj = matmul_scaled(h, big["w_in"], l, ssq, BF16, tm=MXU_TILE, tn=MXU_TILE, n_out=MAIN_W)
    proj3 = proj.reshape(b, s_len, proj.shape[1])

    oa = na_attention(proj3, na_bias)

    (ob,) = banded_attention(
        proj3, cos128, sin128, dil=1, hw=SW_WINDOW, n_heads=SW_HEADS, q_blk=QB_BLK, k_blk=KB_BLK, v_blk=VB_BLK,
        kv_group=SW_GROUP, sink=sink,
    )

    outs, lses = [], []
    for gi, (win, dil) in enumerate(DIL_PAIRS):
        o_g, lse_g = banded_attention(
            proj3, cos128, sin128, dil=dil, hw=(win // 2) // dil, n_heads=DIL_HPG, q_blk=QC_BLK + gi * DIL_HPG,
            k_blk=KC_BLK + gi * DIL_HPG, v_blk=VC_BLK + gi * DIL_HPG, kv_group=1, want_lse=True,
        )
        outs.append(o_g.reshape(t, DIL_HPG * LANE))
        lses.append(lse_g.reshape(t, DIL_HPG * LANE))
    oc = merge_dilated(outs, lses)

    q, k, v = mla_pre(
        h, ssq, lw["w_mla"], mla_q_norm.reshape(1, MLA_Q_RANK), lw["gkv"], lw["wuq"], lw["wukv"], cos64, sin64, s_len
    )
    od = mla_attention(q.reshape(b, s_len, -1), k.reshape(b, s_len, -1), v.reshape(b, s_len, -1))

    mix = jnp.concatenate([oa.reshape(t, A_W), ob.reshape(t, B_QW), oc, od.reshape(t, D_OUT_W)], axis=-1)
    x, h2, ssq2 = matmul_residual(mix, big["w_out"], l, x, g_next=ffn_norm)

    act = up_conv_gate(h2, ssq2, big["w_up"], l, lw["conv_w"], lw["conv_b"], s_len)
    half = MXU_TILE // 2
    if next_attn_norm is None:
        return matmul_residual(act, big["w_down"], l, x, tm=half, tn=half), None, None
    return matmul_residual(act, big["w_down"], l, x, g_next=next_attn_norm, tm=half, tn=half)


def trunk(x3, big, layers, na_bias, attn_norm, sink, mla_q_norm, ffn_norm, final_norm):
    b, s_len, d = x3.shape
    tabs = rope_tables(s_len, HEAD_DIM) + rope_tables(s_len, MLA_ROPE)
    x = x3.reshape(b * s_len, d)
    h, ssq = prenorm_rows(x, attn_norm[0])
    for l, lw in enumerate(layers):
        next_attn_norm = attn_norm[l + 1] if l + 1 < len(layers) else None
        x, h, ssq = encoder_layer(
            x, h, ssq, b, s_len, big, l, lw, na_bias[l], sink[l], mla_q_norm[l], ffn_norm[l], next_attn_norm, tabs
        )
    return rms_rows(x, final_norm, F32).reshape(b, s_len, d)


def kernel(x_prompt, x_sample, attn_norm, w_in, na_rpb, sink, mla_q_norm, w_uq, mla_kv_norm, w_ukv, w_out, ffn_norm, w_up, conv_w, conv_b, w_down, final_norm):
    depth = w_in.shape[0]
    assert w_down.shape[1] % FF_HALF == 0, "d_ff must be a whole number of gate column blocks"
    big = dict(w_in=w_in.astype(BF16), w_out=w_out.astype(BF16), w_up=w_up.astype(BF16), w_down=w_down.astype(BF16))
    layers = [
        prepare_layer(big["w_in"][l, :, MAIN_W:], w_uq[l], mla_kv_norm[l], w_ukv[l], conv_w[l], conv_b[l])
        for l in range(depth)
    ]
    na_bias = na_bias_tables(na_rpb)
    return tuple(
        trunk(x3, big, layers, na_bias, attn_norm, sink, mla_q_norm, ffn_norm, final_norm)
        for x3 in (x_prompt, x_sample)
    )
```

```python
import functools

import numpy as np
import jax
import jax.numpy as jnp
from jax import lax
from jax.experimental import pallas as pl
from jax.experimental.pallas import tpu as pltpu

F32 = jnp.float32
BF16 = jnp.bfloat16

HEAD_DIM = 128
ROPE_THETA = 10000.0
NORM_EPS = 1e-6
NEG_INF = -1e30
GRID_W = 64
NA_HEADS = 8
NA_ROWS = 8
NA_COLS = 16
SW_HEADS = 8
SW_KV_HEADS = 2
SW_GROUP = SW_HEADS // SW_KV_HEADS
SW_WINDOW = 128
DIL_PAIRS = ((128, 1), (512, 4), (2048, 16))
DIL_HPG = 3
DIL_HEADS = DIL_HPG * len(DIL_PAIRS)
MLA_HEADS = 7
MLA_Q_RANK = 896
MLA_KV_RANK = 320
MLA_NOPE = 128
MLA_ROPE = 64
MLA_V = 128

A_W = NA_HEADS * HEAD_DIM
B_QW = SW_HEADS * HEAD_DIM
B_KW = SW_KV_HEADS * HEAD_DIM
C_W = DIL_HEADS * HEAD_DIM
D_OUT_W = MLA_HEADS * MLA_V
MAIN_W = 3 * A_W + B_QW + 2 * B_KW + 3 * C_W

LANE = 128
SUBLANE = 8
MXU_TILE = 1024
VMEM_LIMIT = 56 * 1024 * 1024

QA_BLK, KA_BLK, VA_BLK = 0, A_W // LANE, 2 * A_W // LANE
QB_BLK = 3 * A_W // LANE
KB_BLK = QB_BLK + B_QW // LANE
VB_BLK = KB_BLK + B_KW // LANE
QC_BLK = VB_BLK + B_KW // LANE
KC_BLK = QC_BLK + C_W // LANE
VC_BLK = KC_BLK + C_W // LANE

KV_PAD = 384
MLA_IN_W = MLA_Q_RANK + KV_PAD + LANE
MLA_QK = 2 * LANE

_NT = (((1,), (1,)), ((), ()))
LOG2_E = 1.4426950408889634
LN_2 = 0.6931471805599453
MLA_Q_SCALE = (MLA_NOPE + MLA_ROPE) ** -0.5 * LOG2_E


def _round_up(x, m):
    return (x + m - 1) // m * m


def _params(sem, vmem=VMEM_LIMIT):
    return pltpu.CompilerParams(dimension_semantics=sem, vmem_limit_bytes=vmem)


def _rms_kernel(x_ref, g_ref, o_ref):
    x = x_ref[...]
    ms = jnp.mean(x * x, axis=-1, keepdims=True)
    o_ref[...] = ((x * lax.rsqrt(ms + NORM_EPS)) * g_ref[...]).astype(o_ref.dtype)


def rms_rows(x, g, out_dtype):
    t, d = x.shape
    tm = min(256, t)
    return pl.pallas_call(
        _rms_kernel,
        out_shape=jax.ShapeDtypeStruct((t, d), out_dtype),
        grid=(t // tm,),
        in_specs=[pl.BlockSpec((tm, d), lambda i: (i, 0)), pl.BlockSpec((1, d), lambda i: (0, 0))],
        out_specs=pl.BlockSpec((tm, d), lambda i: (i, 0)),
        compiler_params=_params(("parallel",)),
        name="rms_rows",
    )(x, g.reshape(1, d))


def _row_scale(ssq_ref, d):
    return lax.rsqrt(jnp.sum(ssq_ref[...], axis=-1, keepdims=True) * (1.0 / d) + NORM_EPS)


def _lane_partial_sumsq(x):
    xx = x * x
    part = xx[:, :LANE]
    for c in range(1, x.shape[1] // LANE):
        part = part + xx[:, c * LANE : (c + 1) * LANE]
    return part


def _prenorm_kernel(x_ref, g_ref, h_ref, ssq_ref):
    x = x_ref[...]
    h_ref[...] = (x * g_ref[...]).astype(h_ref.dtype)
    ssq_ref[...] = _lane_partial_sumsq(x)


def prenorm_rows(x, g):
    t, d = x.shape
    tm = min(256, t)
    return pl.pallas_call(
        _prenorm_kernel,
        out_shape=[jax.ShapeDtypeStruct((t, d), BF16), jax.ShapeDtypeStruct((t, LANE), F32)],
        grid=(t // tm,),
        in_specs=[pl.BlockSpec((tm, d), lambda i: (i, 0)), pl.BlockSpec((1, d), lambda i: (0, 0))],
        out_specs=[pl.BlockSpec((tm, d), lambda i: (i, 0)), pl.BlockSpec((tm, LANE), lambda i: (i, 0))],
        compiler_params=_params(("parallel",)),
        name="prenorm_rows",
    )(x, g.reshape(1, d))


def _largest_tile(n, candidates):
    return next(c for c in candidates if n % c == 0)


def _weight_spec(layer, k, tn, col=lambda j: j):
    return pl.BlockSpec((None, k, tn), lambda i, j: (layer, 0, col(j)))


def _mm_scaled_kernel(a_ref, b_ref, ssq_ref, o_ref, *, d):
    acc = jnp.dot(a_ref[...], b_ref[...], preferred_element_type=F32)
    o_ref[...] = (acc * _row_scale(ssq_ref, d)).astype(o_ref.dtype)


def matmul_scaled(a, b, layer, ssq, out_dtype, tm, tn, n_out=None):
    m, k = a.shape
    tm = min(tm, m)
    if n_out is None:
        n = b.shape[2]
        tn = _largest_tile(n, (tn, 512, 256, LANE))
    else:
        n = _round_up(n_out, tn)
        assert n <= b.shape[2]
    return pl.pallas_call(
        functools.partial(_mm_scaled_kernel, d=k),
        out_shape=jax.ShapeDtypeStruct((m, n), out_dtype),
        grid=(m // tm, n // tn),
        in_specs=[
            pl.BlockSpec((tm, k), lambda i, j: (i, 0)),
            _weight_spec(layer, k, tn),
            pl.BlockSpec((tm, LANE), lambda i, j: (i, 0)),
        ],
        out_specs=pl.BlockSpec((tm, tn), lambda i, j: (i, j)),
        compiler_params=_params(("parallel", "parallel")),
        name="matmul_scaled",
    )(a, b, ssq)


def _mm_res_kernel(a_ref, b_ref, r_ref, o_ref):
    o_ref[...] = r_ref[...] + jnp.dot(a_ref[...], b_ref[...], preferred_element_type=F32)


def _mm_res_prenorm_kernel(a_ref, b_ref, r_ref, g_ref, o_ref, h_ref, ssq_ref):
    x = r_ref[...] + jnp.dot(a_ref[...], b_ref[...], preferred_element_type=F32)
    o_ref[...] = x
    h_ref[...] = (x * g_ref[...]).astype(h_ref.dtype)

    @pl.when(pl.program_id(1) == 0)
    def _():
        ssq_ref[...] = jnp.zeros_like(ssq_ref)

    ssq_ref[...] += _lane_partial_sumsq(x)


def matmul_residual(a, b, layer, res, g_next=None, tm=MXU_TILE, tn=MXU_TILE // 2):
    m, k = a.shape
    n = b.shape[2]
    tm, tn = min(tm, m), min(tn, n)
    in_specs = [
        pl.BlockSpec((tm, k), lambda i, j: (i, 0)),
        _weight_spec(layer, k, tn),
        pl.BlockSpec((tm, tn), lambda i, j: (i, j)),
    ]
    tile = pl.BlockSpec((tm, tn), lambda i, j: (i, j))
    if g_next is None:
        return pl.pallas_call(
            _mm_res_kernel,
            out_shape=jax.ShapeDtypeStruct((m, n), F32),
            grid=(m // tm, n // tn),
            in_specs=in_specs,
            out_specs=tile,
            compiler_params=_params(("parallel", "parallel")),
            name="matmul_residual",
        )(a, b, res)
    return pl.pallas_call(
        _mm_res_prenorm_kernel,
        out_shape=[
            jax.ShapeDtypeStruct((m, n), F32),
            jax.ShapeDtypeStruct((m, n), BF16),
            jax.ShapeDtypeStruct((m, LANE), F32),
        ],
        grid=(m // tm, n // tn),
        in_specs=in_specs + [pl.BlockSpec((1, tn), lambda i, j: (0, j))],
        out_specs=[tile, tile, pl.BlockSpec((tm, LANE), lambda i, j: (i, 0))],
        compiler_params=_params(("parallel", "arbitrary")),
        name="matmul_residual_prenorm",
    )(a, b, res, g_next.reshape(1, n))


FF_BLOCK = 512
FF_HALF = FF_BLOCK // 2
EDGE_ROWS = 2 * SUBLANE


def _conv3_gate(prev, cur, nxt, w, b):
    y = prev * w[0:1] + cur * w[1:2] + nxt * w[2:3] + b
    gate, up = y[:, :FF_HALF], y[:, FF_HALF:]
    return (gate / (1.0 + jnp.exp2(gate * (-LOG2_E)))) * up


def _up_conv_gate_kernel(h_ref, wg_ref, wu_ref, ssq_ref, cwg_ref, cwu_ref, cbg_ref, cbu_ref, act_ref, edge_ref, *, d):
    tm = h_ref.shape[0]
    scale = _row_scale(ssq_ref, d)
    row = lax.broadcasted_iota(jnp.int32, (tm, FF_HALF), 0)

    def conv_branch(w_ref, cw_ref, cb_ref, lane0):
        u = jnp.dot(h_ref[...], w_ref[...], preferred_element_type=F32) * scale
        edge_ref[0:SUBLANE, lane0 : lane0 + FF_HALF] = u[0:SUBLANE]
        edge_ref[SUBLANE:, lane0 : lane0 + FF_HALF] = u[tm - SUBLANE : tm]
        prev = jnp.where(row == 0, 0.0, pltpu.roll(u, 1, 0))
        nxt = jnp.where(row == tm - 1, 0.0, pltpu.roll(u, tm - 1, 0))
        w = cw_ref[...]
        return prev * w[0:1] + u * w[1:2] + nxt * w[2:3] + cb_ref[...]

    gate = conv_branch(wg_ref, cwg_ref, cbg_ref, 0)
    up = conv_branch(wu_ref, cwu_ref, cbu_ref, FF_HALF)
    act_ref[...] = ((gate / (1.0 + jnp.exp2(gate * (-LOG2_E)))) * up).astype(act_ref.dtype)


def _edge_fix_kernel(act_ref, lo_ref, hi_ref, cw_ref, cb_ref, o_ref):
    side = pl.program_id(1)
    rows = act_ref.shape[0]
    last = side == 0
    e = EDGE_ROWS
    prev = jnp.where(last, lo_ref[e - 2 : e - 1, :], lo_ref[e - 1 : e, :])
    cur = jnp.where(last, lo_ref[e - 1 : e, :], hi_ref[0:1, :])
    nxt = jnp.where(last, hi_ref[0:1, :], hi_ref[1:2, :])
    w, b = cw_ref[...], cb_ref[...]
    pieces = []
    for j in range(w.shape[1] // FF_BLOCK):
        sl = slice(j * FF_BLOCK, (j + 1) * FF_BLOCK)
        pieces.append(_conv3_gate(prev[:, sl], cur[:, sl], nxt[:, sl], w[:, sl], b[:, sl]))
    new_row = jnp.concatenate(pieces, axis=-1)
    row = lax.broadcasted_iota(jnp.int32, act_ref.shape, 0)
    target = jnp.where(last, rows - 1, 0)
    o_ref[...] = jnp.where(row == target, new_row, act_ref[...].astype(F32)).astype(o_ref.dtype)


def up_conv_gate(h, ssq, w, layer, conv_w, conv_b, s_len):
    t, d = h.shape
    f = w.shape[2] // 2
    tm = min(2 * MXU_TILE, s_len)
    n_i, n_j = t // tm, f // FF_HALF
    act, edges = pl.pallas_call(
        functools.partial(_up_conv_gate_kernel, d=d),
        out_shape=[jax.ShapeDtypeStruct((t, f), BF16), jax.ShapeDtypeStruct((n_i * EDGE_ROWS, 2 * f), F32)],
        grid=(n_i, n_j),
        in_specs=[
            pl.BlockSpec((tm, d), lambda i, j: (i, 0)),
            _weight_spec(layer, d, FF_HALF),
            _weight_spec(layer, d, FF_HALF, lambda j: j + n_j),
            pl.BlockSpec((tm, LANE), lambda i, j: (i, 0)),
            pl.BlockSpec((3, FF_HALF), lambda i, j: (0, j)),
            pl.BlockSpec((3, FF_HALF), lambda i, j: (0, j + n_j)),
            pl.BlockSpec((1, FF_HALF), lambda i, j: (0, j)),
            pl.BlockSpec((1, FF_HALF), lambda i, j: (0, j + n_j)),
        ],
        out_specs=[
            pl.BlockSpec((tm, FF_HALF), lambda i, j: (i, j)),
            pl.BlockSpec((EDGE_ROWS, FF_BLOCK), lambda i, j: (i, j)),
        ],
        compiler_params=_params(("parallel", "parallel")),
        name="up_conv_gate",
    )(h, w, w, ssq, conv_w, conv_w, conv_b, conv_b)

    tiles_per_seq = s_len // tm
    if tiles_per_seq == 1:
        return act
    conv_w, conv_b = _gate_up_blocks(conv_w), _gate_up_blocks(conv_b)
    inner = tiles_per_seq - 1
    fix_rows = 2 * SUBLANE
    lo_tile = lambda bd: (bd // inner) * tiles_per_seq + bd % inner
    act_blk = lambda bd, side: ((lo_tile(bd) + 1) * (tm // fix_rows) - 1 + side, 0)
    return pl.pallas_call(
        _edge_fix_kernel,
        out_shape=jax.ShapeDtypeStruct((t, f), BF16),
        grid=((t // s_len) * inner, 2),
        in_specs=[
            pl.BlockSpec((fix_rows, f), act_blk),
            pl.BlockSpec((EDGE_ROWS, 2 * f), lambda bd, side: (lo_tile(bd), 0)),
            pl.BlockSpec((EDGE_ROWS, 2 * f), lambda bd, side: (lo_tile(bd) + 1, 0)),
            pl.BlockSpec((3, 2 * f), lambda bd, side: (0, 0)),
            pl.BlockSpec((1, 2 * f), lambda bd, side: (0, 0)),
        ],
        out_specs=pl.BlockSpec((fix_rows, f), act_blk),
        input_output_aliases={0: 0},
        compiler_params=_params(("arbitrary", "arbitrary")),
        name="conv_edge_fix",
    )(act, edges, edges, conv_w, conv_b)


def _rope(x, cos, sin_signed):
    return x * cos + pltpu.roll(x, LANE // 2, 1) * sin_signed


def rope_tables(s_len, d):
    inv = 1.0 / (ROPE_THETA ** (jnp.arange(0, d, 2, dtype=F32) / d))
    ang = jnp.arange(s_len, dtype=F32)[:, None] * inv[None, :]
    cos, sin = jnp.cos(ang), jnp.sin(ang)
    pad = jnp.zeros((s_len, LANE // 2 - d // 2), F32)
    return (jnp.concatenate([cos, pad, cos, pad], axis=-1), jnp.concatenate([-sin, pad, sin, pad], axis=-1))


NA_GROUP = 4
NA_WIN_ROWS = 12
NA_FIRST, NA_INTERIOR, NA_LAST = 0, 1, 2


def _na_kernel(q_ref, k_ref, v_ref, bias_ref, o_ref, *, rows, unroll):
    scale = HEAD_DIM**-0.5
    n_groups = rows // NA_GROUP
    gq = NA_GROUP * GRID_W
    wk = NA_WIN_ROWS * GRID_W

    def one_group(g):
        r0 = g * NA_GROUP
        ws = jnp.clip(r0 - NA_ROWS // 2, 0, rows - NA_WIN_ROWS)
        variant = jnp.where(g == 0, NA_FIRST, jnp.where(g == n_groups - 1, NA_LAST, NA_INTERIOR))
        q0 = pl.multiple_of(r0 * GRID_W, gq)
        k0 = pl.multiple_of(ws * GRID_W, GRID_W)
        s = lax.dot_general(q_ref[pl.ds(q0, gq), :], k_ref[pl.ds(k0, wk), :], _NT, preferred_element_type=F32)
        s = s * (scale * LOG2_E) + bias_ref[variant]
        m = jnp.max(s, axis=-1, keepdims=True)
        p = jnp.exp2(s - m)
        l = jnp.sum(p, axis=-1, keepdims=True)
        o = jnp.dot(p.astype(BF16), v_ref[pl.ds(k0, wk), :], preferred_element_type=F32) / l
        o_ref[pl.ds(q0, gq), :] = o.astype(o_ref.dtype)

    def body(it, carry):
        for u in range(unroll):
            one_group(it * unroll + u)
        return carry

    lax.fori_loop(0, n_groups // unroll, body, 0)


def na_bias_tables(rpb_all):
    qc = np.arange(GRID_W)[:, None]
    kc = np.arange(GRID_W)[None, :]
    q_ws = np.clip(qc - NA_COLS // 2, 0, GRID_W - NA_COLS)
    valid = (kc >= q_ws) & (kc < q_ws + NA_COLS)
    coff = np.clip(kc - qc + NA_COLS - 1, 0, 2 * NA_COLS - 2)
    onehot = (coff[None] == np.arange(2 * NA_COLS - 1)[:, None, None]).astype(np.float32)
    e = jnp.einsum("lhrc,cqk->lhrqk", rpb_all.astype(F32), onehot, precision=lax.Precision.HIGHEST)
    e = jnp.where(valid, e * LOG2_E, NEG_INF)
    pad = NA_WIN_ROWS
    e_pad = jnp.pad(e, ((0, 0), (0, 0), (pad, pad), (0, 0), (0, 0)))
    i = np.arange(NA_WIN_ROWS)
    slabs, valid_rows = [], []
    for r0_minus_ws, first_valid in ((0, lambda u: 0), (NA_ROWS // 2, lambda u: u), (NA_ROWS, lambda u: NA_ROWS // 2)):
        for u in range(NA_GROUP):
            lo = pad + (NA_ROWS - 1) - u - r0_minus_ws
            slabs.append(e_pad[:, :, lo : lo + NA_WIN_ROWS])
            valid_rows.append((i >= first_valid(u)) & (i < first_valid(u) + NA_ROWS))
    tab = jnp.stack(slabs, axis=2)
    valid_rows = np.stack(valid_rows)[:, :, None, None]
    tab = jnp.where(valid_rows, tab, NEG_INF)
    tab = tab.transpose(0, 1, 2, 4, 3, 5)
    return tab.reshape(tab.shape[:2] + (3, NA_GROUP * GRID_W, NA_WIN_ROWS * GRID_W))


def na_attention(proj3, bias_tab):
    b, s_len, _ = proj3.shape
    rows = s_len // GRID_W
    assert rows >= NA_WIN_ROWS and rows % NA_GROUP == 0, "grid too short for the grouped neighbourhood kernel"
    n_groups = rows // NA_GROUP
    blk = lambda off: pl.BlockSpec((None, s_len, LANE), lambda bi, h, off=off: (bi, 0, off + h))
    return pl.pallas_call(
        functools.partial(_na_kernel, rows=rows, unroll=4 if n_groups % 4 == 0 else 1),
        out_shape=jax.ShapeDtypeStruct((b, s_len, A_W), BF16),
        grid=(b, NA_HEADS),
        in_specs=[
            blk(QA_BLK),
            blk(KA_BLK),
            blk(VA_BLK),
            pl.BlockSpec((None, 3, NA_GROUP * GRID_W, NA_WIN_ROWS * GRID_W), lambda bi, h: (h, 0, 0, 0)),
        ],
        out_specs=pl.BlockSpec((None, s_len, LANE), lambda bi, h: (bi, 0, h)),
        compiler_params=_params(("parallel", "parallel")),
        name="na_attention",
    )(proj3, proj3, proj3, bias_tab)


def _band_kernel(*refs, s_len, dil, tq, hw, win, has_sink, want_lse, unroll):
    refs = list(refs)
    q_ref, k_ref, v_ref, cos_ref, sin_ref, mask_ref = refs[:6]
    del refs[:6]
    sink_ref = refs.pop(0) if has_sink else None
    o_ref = refs.pop(0)
    lse_ref = refs.pop(0) if want_lse else None
    stage, qd, kd, vd = refs[:4]
    od = refs[4] if dil > 1 else None
    ld = refs[5] if (dil > 1 and want_lse) else None
    sub = s_len // dil
    nq = sub // tq
    scale = HEAD_DIM**-0.5

    two_pass = dil > 4 and dil % 4 == 0
    stage2 = refs[-1] if two_pass else None
    quarter = s_len // 4
    d2 = dil // 4

    def split_classes(put):
        if not two_pass:
            for r in range(dil):
                put(r, stage[pl.ds(r, sub, stride=dil), :])
            return
        for r1 in range(4):
            stage2[pl.ds(r1 * quarter, quarter), :] = stage[pl.ds(r1, quarter, stride=4), :]
        for r1 in range(4):
            for r2 in range(d2):
                put(r1 + 4 * r2, stage2[pl.ds(r1 * quarter + r2, sub, stride=d2), :])

    def join_classes(get, dst_ref):
        if not two_pass:
            for r in range(dil):
                dst_ref[pl.ds(r, sub, stride=dil), :] = get(r)
            return
        for r1 in range(4):
            for r2 in range(d2):
                stage2[pl.ds(r1 * quarter + r2, sub, stride=d2), :] = get(r1 + 4 * r2)
        for r1 in range(4):
            dst_ref[pl.ds(r1, quarter, stride=4), :] = stage2[pl.ds(r1 * quarter, quarter), :]

    def regroup(val, dst):
        if dil == 1:
            dst[0] = val.astype(BF16)
        else:
            stage[...] = val

            def put(r, rows):
                dst[r] = rows.astype(BF16)

            split_classes(put)

    cos, sin = cos_ref[...], sin_ref[...]
    regroup(_rope(k_ref[...].astype(F32), cos, sin), kd)
    regroup(_rope(q_ref[...].astype(F32), cos, sin), qd)
    regroup(v_ref[...].astype(F32), vd)

    def one_block(it):
        r = it // nq
        q0 = pl.multiple_of((it % nq) * tq, tq)
        ws = pl.multiple_of(jnp.clip(q0 - hw, 0, sub - win), 64)
        s = lax.dot_general(qd[r, pl.ds(q0, tq), :], kd[r, pl.ds(ws, win), :], _NT, preferred_element_type=F32)
        s = s * (scale * LOG2_E) + mask_ref[(q0 - ws) // hw]
        m = jnp.max(s, axis=-1, keepdims=True)
        if has_sink:
            sk = sink_ref[0:1, 0:1] * LOG2_E
            m = jnp.maximum(m, sk)
        p = jnp.exp2(s - m)
        l = jnp.sum(p, axis=-1, keepdims=True)
        if has_sink:
            l = l + jnp.exp2(sk - m)
        o = jnp.dot(p.astype(BF16), vd[r, pl.ds(ws, win), :], preferred_element_type=F32) / l
        if dil == 1:
            o_ref[pl.ds(q0, tq), :] = o.astype(o_ref.dtype)
        else:
            od[r, pl.ds(q0, tq), :] = o
        if want_lse:
            lse = jnp.broadcast_to(m * LN_2 + jnp.log(l), (tq, LANE))
            if dil == 1:
                lse_ref[pl.ds(q0, tq), :] = lse
            else:
                ld[r, pl.ds(q0, tq), :] = lse

    def body(g, carry):
        for u in range(unroll):
            one_block(g * unroll + u)
        return carry

    lax.fori_loop(0, dil * nq // unroll, body, 0)

    if dil > 1:
        join_classes(lambda r: od[r], stage)
        o_ref[...] = stage[...].astype(o_ref.dtype)
        if want_lse:
            join_classes(lambda r: ld[r], lse_ref)


def banded_attention(proj3, cos, sin, *, dil, hw, n_heads, q_blk, k_blk, v_blk, kv_group, sink=None, want_lse=False):
    b, s_len, _ = proj3.shape
    sub = s_len // dil
    tq = min(max(128, 2 * hw), sub)
    win = min(tq + 2 * hw, sub)
    total = dil * (sub // tq)
    unroll = 4 if total % 4 == 0 else 1
    out_w = n_heads * LANE

    def col(off, group=1):
        return pl.BlockSpec((None, s_len, LANE), lambda bi, h: (bi, 0, off + h // group))

    offsets = [q0 - min(max(q0 - hw, 0), sub - win) for q0 in range(0, sub, tq)]
    rel = np.arange(win)[None, :] - np.arange(tq)[:, None]
    band = np.stack([np.where(np.abs(rel - off) <= hw, 0.0, NEG_INF) for off in range(0, max(offsets) + 1, hw)])
    band = jnp.asarray(band, F32)

    table = pl.BlockSpec((s_len, LANE), lambda bi, h: (0, 0))
    in_specs = [col(q_blk), col(k_blk, kv_group), col(v_blk, kv_group), table, table]
    in_specs.append(pl.BlockSpec(band.shape, lambda bi, h: (0, 0, 0)))
    args = [proj3, proj3, proj3, cos, sin, band]
    if sink is not None:
        in_specs.append(pl.BlockSpec((None, SUBLANE, LANE), lambda bi, h: (h, 0, 0)))
        args.append(jnp.broadcast_to(sink.astype(F32)[:, None, None], (n_heads, SUBLANE, LANE)))
    out_spec = pl.BlockSpec((None, s_len, LANE), lambda bi, h: (bi, 0, h))
    out_shape = [jax.ShapeDtypeStruct((b, s_len, out_w), BF16)]
    out_specs = [out_spec]
    if want_lse:
        out_shape.append(jax.ShapeDtypeStruct((b, s_len, out_w), F32))
        out_specs.append(out_spec)
    slab = (dil, sub, LANE)
    scratch = [pltpu.VMEM((s_len, LANE), F32), pltpu.VMEM(slab, BF16), pltpu.VMEM(slab, BF16), pltpu.VMEM(slab, BF16)]
    if dil > 1:
        scratch.append(pltpu.VMEM(slab, F32))
        if want_lse:
            scratch.append(pltpu.VMEM(slab, F32))
    if dil > 4 and dil % 4 == 0:
        scratch.append(pltpu.VMEM((s_len, LANE), F32))
    return pl.pallas_call(
        functools.partial(
            _band_kernel, s_len=s_len, dil=dil, tq=tq, hw=hw, win=win, has_sink=sink is not None,
            want_lse=want_lse, unroll=unroll,
        ),
        out_shape=out_shape,
        grid=(b, n_heads),
        in_specs=in_specs,
        out_specs=out_specs,
        scratch_shapes=scratch,
        compiler_params=_params(("parallel", "arbitrary")),
        name=f"banded_attention_d{dil}",
    )(*args)


def _merge_kernel(o0_ref, o1_ref, o2_ref, l0_ref, l1_ref, l2_ref, out_ref):
    l0, l1, l2 = l0_ref[...], l1_ref[...], l2_ref[...]
    mx = jnp.maximum(jnp.maximum(l0, l1), l2)
    e0, e1, e2 = jnp.exp(l0 - mx), jnp.exp(l1 - mx), jnp.exp(l2 - mx)
    den = e0 + e1 + e2
    gw = DIL_HPG * LANE
    for g, (o_ref, e) in enumerate(((o0_ref, e0), (o1_ref, e1), (o2_ref, e2))):
        out_ref[:, g * gw : (g + 1) * gw] = (o_ref[...].astype(F32) * (e / den)).astype(out_ref.dtype)


def merge_dilated(outs, lses):
    t, gw = outs[0].shape
    tm = min(512, t)
    spec = pl.BlockSpec((tm, gw), lambda i: (i, 0))
    return pl.pallas_call(
        _merge_kernel,
        out_shape=jax.ShapeDtypeStruct((t, len(outs) * gw), BF16),
        grid=(t // tm,),
        in_specs=[spec] * 6,
        out_specs=pl.BlockSpec((tm, len(outs) * gw), lambda i: (i, 0)),
        compiler_params=_params(("parallel",)),
        name="merge_dilated",
    )(*outs, *lses)


def _mla_pre_kernel(h_ref, ssq_ref, w1_ref, gq_ref, gkv_ref, wuq_ref, wukv_ref, cos_ref, sin_ref, q_ref, k_ref, v_ref):
    c = jnp.dot(h_ref[...], w1_ref[...], preferred_element_type=F32) * _row_scale(ssq_ref, h_ref.shape[1])
    cq = c[:, :MLA_Q_RANK]
    ckv = c[:, MLA_Q_RANK : MLA_Q_RANK + KV_PAD]
    kr = c[:, MLA_Q_RANK + KV_PAD :]
    cos, sin = cos_ref[...], sin_ref[...]

    nq = (cq * lax.rsqrt(jnp.mean(cq * cq, axis=-1, keepdims=True) + NORM_EPS)) * gq_ref[...]
    q = jnp.dot(nq.astype(BF16), wuq_ref[...], preferred_element_type=F32) * MLA_Q_SCALE
    ms = jnp.sum(ckv * ckv, axis=-1, keepdims=True) * (1.0 / MLA_KV_RANK)
    nkv = (ckv * lax.rsqrt(ms + NORM_EPS)) * gkv_ref[...]
    kv = jnp.dot(nkv.astype(BF16), wukv_ref[...], preferred_element_type=F32)
    k_rope = _rope(kr, cos, sin).astype(k_ref.dtype)
    for h in range(MLA_HEADS):
        lo = h * MLA_QK
        q_ref[:, lo : lo + LANE] = q[:, lo : lo + LANE].astype(q_ref.dtype)
        q_ref[:, lo + LANE : lo + MLA_QK] = _rope(q[:, lo + LANE : lo + MLA_QK], cos, sin).astype(q_ref.dtype)
        k_ref[:, lo : lo + LANE] = kv[:, h * LANE : (h + 1) * LANE].astype(k_ref.dtype)
        k_ref[:, lo + LANE : lo + MLA_QK] = k_rope
        v_ref[:, lo : lo + LANE] = kv[:, D_OUT_W + h * LANE : D_OUT_W + (h + 1) * LANE].astype(v_ref.dtype)
        v_ref[:, lo + LANE : lo + MLA_QK] = jnp.ones((v_ref.shape[0], LANE), v_ref.dtype)


def mla_pre(h, ssq, w1, gq, gkv, wuq, wukv, cos, sin, s_len):
    t, d = h.shape
    tm = min(256, s_len)
    n_pos = s_len // tm
    const = lambda shape: pl.BlockSpec(shape, lambda i: (0, 0))
    row = lambda w: pl.BlockSpec((tm, w), lambda i: (i, 0))
    return pl.pallas_call(
        _mla_pre_kernel,
        out_shape=[
            jax.ShapeDtypeStruct((t, MLA_HEADS * MLA_QK), BF16),
            jax.ShapeDtypeStruct((t, MLA_HEADS * MLA_QK), BF16),
            jax.ShapeDtypeStruct((t, MLA_HEADS * MLA_QK), BF16),
        ],
        grid=(t // tm,),
        in_specs=[
            row(d),
            row(LANE),
            const((d, MLA_IN_W)),
            const((1, MLA_Q_RANK)),
            const((1, KV_PAD)),
            const((MLA_Q_RANK, MLA_HEADS * MLA_QK)),
            const((KV_PAD, 2 * D_OUT_W)),
            pl.BlockSpec((tm, LANE), lambda i: (i % n_pos, 0)),
            pl.BlockSpec((tm, LANE), lambda i: (i % n_pos, 0)),
        ],
        out_specs=[row(MLA_HEADS * MLA_QK), row(MLA_HEADS * MLA_QK), row(MLA_HEADS * MLA_QK)],
        compiler_params=_params(("parallel",)),
        name="mla_pre",
    )(h, ssq, w1, gq, gkv, wuq, wukv, cos, sin)


def _mla_kernel(q_ref, k_ref, v_ref, o_ref, s_a, s_b, *, tq):
    n_blk = q_ref.shape[0] // tq

    def scores(i, dst):
        q0 = pl.multiple_of(jnp.minimum(i, n_blk - 1) * tq, tq)
        dst[...] = lax.dot_general(q_ref[pl.ds(q0, tq), :], k_ref[...], _NT, preferred_element_type=F32)

    def finish(src, i):
        s = src[...]
        p = jnp.exp2(s - jnp.max(s, axis=-1, keepdims=True))
        ov = jnp.dot(p.astype(BF16), v_ref[...], preferred_element_type=F32)
        o = ov[:, :LANE] / ov[:, LANE : LANE + 1]
        o_ref[pl.ds(pl.multiple_of(i * tq, tq), tq), :] = o.astype(o_ref.dtype)

    scores(0, s_a)

    def pair(g, carry):
        i = 2 * g
        scores(i + 1, s_b)
        finish(s_a, i)
        scores(i + 2, s_a)
        finish(s_b, i + 1)
        return carry

    lax.fori_loop(0, n_blk // 2, pair, 0)


def mla_attention(q, k, v):
    b, s_len, _ = q.shape
    tq = min(512, s_len // 2)
    blk = lambda w: pl.BlockSpec((None, s_len, w), lambda bi, h: (bi, 0, h))
    return pl.pallas_call(
        functools.partial(_mla_kernel, tq=tq),
        out_shape=jax.ShapeDtypeStruct((b, s_len, D_OUT_W), BF16),
        grid=(b, MLA_HEADS),
        in_specs=[blk(MLA_QK), blk(MLA_QK), blk(MLA_QK)],
        out_specs=blk(LANE),
        scratch_shapes=[pltpu.VMEM((tq, s_len), F32), pltpu.VMEM((tq, s_len), F32)],
        compiler_params=_params(("parallel", "parallel")),
        name="mla_attention",
    )(q, k, v)


def _pad_cols(w, width):
    return jnp.pad(w, ((0, 0),) * (w.ndim - 1) + ((0, width - w.shape[-1]),))


def _interleave_rope(w):
    z = jnp.zeros(w.shape[:-1] + (LANE // 2 - MLA_ROPE // 2,), w.dtype)
    return jnp.concatenate([w[..., : MLA_ROPE // 2], z, w[..., MLA_ROPE // 2 :], z], axis=-1)


def _gate_up_blocks(w):
    f = w.shape[-1] // 2
    lead = w.shape[:-1]
    gate = w[..., :f].reshape(lead + (f // FF_HALF, FF_HALF))
    up = w[..., f:].reshape(lead + (f // FF_HALF, FF_HALF))
    return jnp.stack([gate, up], axis=-2).reshape(lead + (2 * f,))


def prepare_layer(w_in_mla, w_uq, mla_kv_norm, w_ukv, conv_w, conv_b):
    w_mla = jnp.concatenate(
        [
            w_in_mla[:, :MLA_Q_RANK],
            _pad_cols(w_in_mla[:, MLA_Q_RANK : MLA_Q_RANK + MLA_KV_RANK], KV_PAD),
            _interleave_rope(w_in_mla[:, MLA_Q_RANK + MLA_KV_RANK :]),
        ],
        axis=-1,
    )
    uq = w_uq.reshape(MLA_Q_RANK, MLA_HEADS, MLA_NOPE + MLA_ROPE)
    wuq = jnp.concatenate([uq[..., :MLA_NOPE], _interleave_rope(uq[..., MLA_NOPE:])], axis=-1)
    wuq = wuq.reshape(MLA_Q_RANK, MLA_HEADS * MLA_QK).astype(BF16)
    ukv = w_ukv.reshape(MLA_KV_RANK, MLA_HEADS, MLA_NOPE + MLA_V)
    wukv = jnp.concatenate(
        [ukv[..., :MLA_NOPE].reshape(MLA_KV_RANK, D_OUT_W), ukv[..., MLA_NOPE:].reshape(MLA_KV_RANK, D_OUT_W)], axis=-1
    )
    wukv = jnp.pad(wukv, ((0, KV_PAD - MLA_KV_RANK), (0, 0))).astype(BF16)
    gkv = _pad_cols(mla_kv_norm.reshape(1, MLA_KV_RANK), KV_PAD)
    return dict(w_mla=w_mla, wuq=wuq, wukv=wukv, gkv=gkv, conv_w=conv_w, conv_b=conv_b[None])


def encoder_layer(x, h, ssq, b, s_len, big, l, lw, na_bias, sink, mla_q_norm, ffn_norm, next_attn_norm, tabs):
    t, d = x.shape
    cos128, sin128, cos64, sin64 = tabs
    proj = matmul_scaled(h, big["w_in"], l, ssq, BF16, tm=MXU_TILE, tn=MXU_TILE, n_out=MAIN_W)
    proj3 = proj.reshape(b, s_len, proj.shape[1])

    oa = na_attention(proj3, na_bias)

    (ob,) = banded_attention(
        proj3, cos128, sin128, dil=1, hw=SW_WINDOW, n_heads=SW_HEADS, q_blk=QB_BLK, k_blk=KB_BLK, v_blk=VB_BLK,
        kv_group=SW_GROUP, sink=sink,
    )

    outs, lses = [], []
    for gi, (win, dil) in enumerate(DIL_PAIRS):
        o_g, lse_g = banded_attention(
            proj3, cos128, sin128, dil=dil, hw=(win // 2) // dil, n_heads=DIL_HPG, q_blk=QC_BLK + gi * DIL_HPG,
            k_blk=KC_BLK + gi * DIL_HPG, v_blk=VC_BLK + gi * DIL_HPG, kv_group=1, want_lse=True,
        )
        outs.append(o_g.reshape(t, DIL_HPG * LANE))
        lses.append(lse_g.reshape(t, DIL_HPG * LANE))
    oc = merge_dilated(outs, lses)

    q, k, v = mla_pre(
        h, ssq, lw["w_mla"], mla_q_norm.reshape(1, MLA_Q_RANK), lw["gkv"], lw["wuq"], lw["wukv"], cos64, sin64, s_len
    )
    od = mla_attention(q.reshape(b, s_len, -1), k.reshape(b, s_len, -1), v.reshape(b, s_len, -1))

    mix = jnp.concatenate([oa.reshape(t, A_W), ob.reshape(t, B_QW), oc, od.reshape(t, D_OUT_W)], axis=-1)
    x, h2, ssq2 = matmul_residual(mix, big["w_out"], l, x, g_next=ffn_norm)

    act = up_conv_gate(h2, ssq2, big["w_up"], l, lw["conv_w"], lw["conv_b"], s_len)
    half = MXU_TILE // 2
    if next_attn_norm is None:
        return matmul_residual(act, big["w_down"], l, x, tm=half, tn=half), None, None
    return matmul_residual(act, big["w_down"], l, x, g_next=next_attn_norm, tm=half, tn=half)


def trunk(x3, big, layers, na_bias, attn_norm, sink, mla_q_norm, ffn_norm, final_norm):
    b, s_len, d = x3.shape
    tabs = rope_tables(s_len, HEAD_DIM) + rope_tables(s_len, MLA_ROPE)
    x = x3.reshape(b * s_len, d)
    h, ssq = prenorm_rows(x, attn_norm[0])
    for l, lw in enumerate(layers):
        next_attn_norm = attn_norm[l + 1] if l + 1 < len(layers) else None
        x, h, ssq = encoder_layer(
            x, h, ssq, b, s_len, big, l, lw, na_bias[l], sink[l], mla_q_norm[l], ffn_norm[l], next_attn_norm, tabs
        )
    return rms_rows(x, final_norm, F32).reshape(b, s_len, d)


def kernel(x_prompt, x_sample, attn_norm, w_in, na_rpb, sink, mla_q_norm, w_uq, mla_kv_norm, w_ukv, w_out, ffn_norm, w_up, conv_w, conv_b, w_down, final_norm):
    depth = w_in.shape[0]
    assert w_down.shape[1] % FF_HALF == 0, "d_ff must be a whole number of gate column blocks"
    big = dict(w_in=w_in.astype(BF16), w_out=w_out.astype(BF16), w_up=w_up.astype(BF16), w_down=w_down.astype(BF16))
    layers = [
        prepare_layer(big["w_in"][l, :, MAIN_W:], w_uq[l], mla_kv_norm[l], w_ukv[l], conv_w[l], conv_b[l])
        for l in range(depth)
    ]
    na_bias = na_bias_tables(na_rpb)
    return tuple(
        trunk(x3, big, layers, na_bias, attn_norm, sink, mla_q_norm, ffn_norm, final_norm)
        for x3 in (x_prompt, x_sample)
    )
```

```python
import functools

import numpy as np
import jax
import jax.numpy as jnp
from jax import lax
from jax.experimental import pallas as pl
from jax.experimental.pallas import tpu as pltpu

F32 = jnp.float32
BF16 = jnp.bfloat16

HEAD_DIM = 128
ROPE_THETA = 10000.0
NORM_EPS = 1e-6
NEG_INF = -1e30
GRID_W = 64
NA_HEADS = 8
NA_ROWS = 8
NA_COLS = 16
SW_HEADS = 8
SW_KV_HEADS = 2
SW_GROUP = SW_HEADS // SW_KV_HEADS
SW_WINDOW = 128
DIL_PAIRS = ((128, 1), (512, 4), (2048, 16))
DIL_HPG = 3
DIL_HEADS = DIL_HPG * len(DIL_PAIRS)
MLA_HEADS = 7
MLA_Q_RANK = 896
MLA_KV_RANK = 320
MLA_NOPE = 128
MLA_ROPE = 64
MLA_V = 128

A_W = NA_HEADS * HEAD_DIM
B_QW = SW_HEADS * HEAD_DIM
B_KW = SW_KV_HEADS * HEAD_DIM
C_W = DIL_HEADS * HEAD_DIM
D_OUT_W = MLA_HEADS * MLA_V
MAIN_W = 3 * A_W + B_QW + 2 * B_KW + 3 * C_W

LANE = 128
SUBLANE = 8
MXU_TILE = 1024
VMEM_LIMIT = 56 * 1024 * 1024

QA_BLK, KA_BLK, VA_BLK = 0, A_W // LANE, 2 * A_W // LANE
QB_BLK = 3 * A_W // LANE
KB_BLK = QB_BLK + B_QW // LANE
VB_BLK = KB_BLK + B_KW // LANE
QC_BLK = VB_BLK + B_KW // LANE
KC_BLK = QC_BLK + C_W // LANE
VC_BLK = KC_BLK + C_W // LANE

KV_PAD = 384
MLA_IN_W = MLA_Q_RANK + KV_PAD + LANE
MLA_QK = 2 * LANE

_NT = (((1,), (1,)), ((), ()))
LOG2_E = 1.4426950408889634
LN_2 = 0.6931471805599453
MLA_Q_SCALE = (MLA_NOPE + MLA_ROPE) ** -0.5 * LOG2_E


def _round_up(x, m):
    return (x + m - 1) // m * m


def _params(sem, vmem=VMEM_LIMIT):
    return pltpu.CompilerParams(dimension_semantics=sem, vmem_limit_bytes=vmem)


def _rms_kernel(x_ref, g_ref, o_ref):
    x = x_ref[...]
    ms = jnp.mean(x * x, axis=-1, keepdims=True)
    o_ref[...] = ((x * lax.rsqrt(ms + NORM_EPS)) * g_ref[...]).astype(o_ref.dtype)


def rms_rows(x, g, out_dtype):
    t, d = x.shape
    tm = min(256, t)
    return pl.pallas_call(
        _rms_kernel,
        out_shape=jax.ShapeDtypeStruct((t, d), out_dtype),
        grid=(t // tm,),
        in_specs=[pl.BlockSpec((tm, d), lambda i: (i, 0)), pl.BlockSpec((1, d), lambda i: (0, 0))],
        out_specs=pl.BlockSpec((tm, d), lambda i: (i, 0)),
        compiler_params=_params(("parallel",)),
        name="rms_rows",
    )(x, g.reshape(1, d))


def _row_scale(ssq_ref, d):
    return lax.rsqrt(jnp.sum(ssq_ref[...], axis=-1, keepdims=True) * (1.0 / d) + NORM_EPS)


def _lane_partial_sumsq(x):
    xx = x * x
    part = xx[:, :LANE]
    for c in range(1, x.shape[1] // LANE):
        part = part + xx[:, c * LANE : (c + 1) * LANE]
    return part


def _prenorm_kernel(x_ref, g_ref, h_ref, ssq_ref):
    x = x_ref[...]
    h_ref[...] = (x * g_ref[...]).astype(h_ref.dtype)
    ssq_ref[...] = _lane_partial_sumsq(x)


def prenorm_rows(x, g):
    t, d = x.shape
    tm = min(256, t)
    return pl.pallas_call(
        _prenorm_kernel,
        out_shape=[jax.ShapeDtypeStruct((t, d), BF16), jax.ShapeDtypeStruct((t, LANE), F32)],
        grid=(t // tm,),
        in_specs=[pl.BlockSpec((tm, d), lambda i: (i, 0)), pl.BlockSpec((1, d), lambda i: (0, 0))],
        out_specs=[pl.BlockSpec((tm, d), lambda i: (i, 0)), pl.BlockSpec((tm, LANE), lambda i: (i, 0))],
        compiler_params=_params(("parallel",)),
        name="prenorm_rows",
    )(x, g.reshape(1, d))


def _largest_tile(n, candidates):
    return next(c for c in candidates if n % c == 0)


def _weight_spec(layer, k, tn, col=lambda j: j):
    return pl.BlockSpec((None, k, tn), lambda i, j: (layer, 0, col(j)))


def _mm_scaled_kernel(a_ref, b_ref, ssq_ref, o_ref, *, d):
    acc = jnp.dot(a_ref[...], b_ref[...], preferred_element_type=F32)
    o_ref[...] = (acc * _row_scale(ssq_ref, d)).astype(o_ref.dtype)


def matmul_scaled(a, b, layer, ssq, out_dtype, tm, tn, n_out=None):
    m, k = a.shape
    tm = min(tm, m)
    if n_out is None:
        n = b.shape[2]
        tn = _largest_tile(n, (tn, 512, 256, LANE))
    else:
        n = _round_up(n_out, tn)
        assert n <= b.shape[2]
    return pl.pallas_call(
        functools.partial(_mm_scaled_kernel, d=k),
        out_shape=jax.ShapeDtypeStruct((m, n), out_dtype),
        grid=(m // tm, n // tn),
        in_specs=[
            pl.BlockSpec((tm, k), lambda i, j: (i, 0)),
            _weight_spec(layer, k, tn),
            pl.BlockSpec((tm, LANE), lambda i, j: (i, 0)),
        ],
        out_specs=pl.BlockSpec((tm, tn), lambda i, j: (i, j)),
        compiler_params=_params(("parallel", "parallel")),
        name="matmul_scaled",
    )(a, b, ssq)


def _mm_res_kernel(a_ref, b_ref, r_ref, o_ref):
    o_ref[...] = r_ref[...] + jnp.dot(a_ref[...], b_ref[...], preferred_element_type=F32)


def _mm_res_prenorm_kernel(a_ref, b_ref, r_ref, g_ref, o_ref, h_ref, ssq_ref):
    x = r_ref[...] + jnp.dot(a_ref[...], b_ref[...], preferred_element_type=F32)
    o_ref[...] = x
    h_ref[...] = (x * g_ref[...]).astype(h_ref.dtype)

    @pl.when(pl.program_id(1) == 0)
    def _():
        ssq_ref[...] = jnp.zeros_like(ssq_ref)

    ssq_ref[...] += _lane_partial_sumsq(x)


def matmul_residual(a, b, layer, res, g_next=None, tm=MXU_TILE, tn=MXU_TILE // 2):
    m, k = a.shape
    n = b.shape[2]
    tm, tn = min(tm, m), min(tn, n)
    in_specs = [
        pl.BlockSpec((tm, k), lambda i, j: (i, 0)),
        _weight_spec(layer, k, tn),
        pl.BlockSpec((tm, tn), lambda i, j: (i, j)),
    ]
    tile = pl.BlockSpec((tm, tn), lambda i, j: (i, j))
    if g_next is None:
        return pl.pallas_call(
            _mm_res_kernel,
            out_shape=jax.ShapeDtypeStruct((m, n), F32),
            grid=(m // tm, n // tn),
            in_specs=in_specs,
            out_specs=tile,
            compiler_params=_params(("parallel", "parallel")),
            name="matmul_residual",
        )(a, b, res)
    return pl.pallas_call(
        _mm_res_prenorm_kernel,
        out_shape=[
            jax.ShapeDtypeStruct((m, n), F32),
            jax.ShapeDtypeStruct((m, n), BF16),
            jax.ShapeDtypeStruct((m, LANE), F32),
        ],
        grid=(m // tm, n // tn),
        in_specs=in_specs + [pl.BlockSpec((1, tn), lambda i, j: (0, j))],
        out_specs=[tile, tile, pl.BlockSpec((tm, LANE), lambda i, j: (i, 0))],
        compiler_params=_params(("parallel", "arbitrary")),
        name="matmul_residual_prenorm",
    )(a, b, res, g_next.reshape(1, n))


FF_BLOCK = 512
FF_HALF = FF_BLOCK // 2
EDGE_ROWS = 2 * SUBLANE


def _conv3_gate(prev, cur, nxt, w, b):
    y = prev * w[0:1] + cur * w[1:2] + nxt * w[2:3] + b
    gate, up = y[:, :FF_HALF], y[:, FF_HALF:]
    return (gate / (1.0 + jnp.exp2(gate * (-LOG2_E)))) * up


def _up_conv_gate_kernel(h_ref, wg_ref, wu_ref, ssq_ref, cwg_ref, cwu_ref, cbg_ref, cbu_ref, act_ref, edge_ref, *, d):
    tm = h_ref.shape[0]
    scale = _row_scale(ssq_ref, d)
    row = lax.broadcasted_iota(jnp.int32, (tm, FF_HALF), 0)

    def conv_branch(w_ref, cw_ref, cb_ref, lane0):
        u = jnp.dot(h_ref[...], w_ref[...], preferred_element_type=F32) * scale
        edge_ref[0:SUBLANE, lane0 : lane0 + FF_HALF] = u[0:SUBLANE]
        edge_ref[SUBLANE:, lane0 : lane0 + FF_HALF] = u[tm - SUBLANE : tm]
        prev = jnp.where(row == 0, 0.0, pltpu.roll(u, 1, 0))
        nxt = jnp.where(row == tm - 1, 0.0, pltpu.roll(u, tm - 1, 0))
        w = cw_ref[...]
        return prev * w[0:1] + u * w[1:2] + nxt * w[2:3] + cb_ref[...]

    gate = conv_branch(wg_ref, cwg_ref, cbg_ref, 0)
    up = conv_branch(wu_ref, cwu_ref, cbu_ref, FF_HALF)
    act_ref[...] = ((gate / (1.0 + jnp.exp2(gate * (-LOG2_E)))) * up).astype(act_ref.dtype)


def _edge_fix_kernel(act_ref, lo_ref, hi_ref, cw_ref, cb_ref, o_ref):
    side = pl.program_id(1)
    rows = act_ref.shape[0]
    last = side == 0
    e = EDGE_ROWS
    prev = jnp.where(last, lo_ref[e - 2 : e - 1, :], lo_ref[e - 1 : e, :])
    cur = jnp.where(last, lo_ref[e - 1 : e, :], hi_ref[0:1, :])
    nxt = jnp.where(last, hi_ref[0:1, :], hi_ref[1:2, :])
    w, b = cw_ref[...], cb_ref[...]
    pieces = []
    for j in range(w.shape[1] // FF_BLOCK):
        sl = slice(j * FF_BLOCK, (j + 1) * FF_BLOCK)
        pieces.append(_conv3_gate(prev[:, sl], cur[:, sl], nxt[:, sl], w[:, sl], b[:, sl]))
    new_row = jnp.concatenate(pieces, axis=-1)
    row = lax.broadcasted_iota(jnp.int32, act_ref.shape, 0)
    target = jnp.where(last, rows - 1, 0)
    o_ref[...] = jnp.where(row == target, new_row, act_ref[...].astype(F32)).astype(o_ref.dtype)


def up_conv_gate(h, ssq, w, layer, conv_w, conv_b, s_len):
    t, d = h.shape
    f = w.shape[2] // 2
    tm = min(2 * MXU_TILE, s_len)
    n_i, n_j = t // tm, f // FF_HALF
    act, edges = pl.pallas_call(
        functools.partial(_up_conv_gate_kernel, d=d),
        out_shape=[jax.ShapeDtypeStruct((t, f), BF16), jax.ShapeDtypeStruct((n_i * EDGE_ROWS, 2 * f), F32)],
        grid=(n_i, n_j),
        in_specs=[
            pl.BlockSpec((tm, d), lambda i, j: (i, 0)),
            _weight_spec(layer, d, FF_HALF),
            _weight_spec(layer, d, FF_HALF, lambda j: j + n_j),
            pl.BlockSpec((tm, LANE), lambda i, j: (i, 0)),
            pl.BlockSpec((3, FF_HALF), lambda i, j: (0, j)),
            pl.BlockSpec((3, FF_HALF), lambda i, j: (0, j + n_j)),
            pl.BlockSpec((1, FF_HALF), lambda i, j: (0, j)),
            pl.BlockSpec((1, FF_HALF), lambda i, j: (0, j + n_j)),
        ],
        out_specs=[
            pl.BlockSpec((tm, FF_HALF), lambda i, j: (i, j)),
            pl.BlockSpec((EDGE_ROWS, FF_BLOCK), lambda i, j: (i, j)),
        ],
        compiler_params=_params(("parallel", "parallel")),
        name="up_conv_gate",
    )(h, w, w, ssq, conv_w, conv_w, conv_b, conv_b)

    tiles_per_seq = s_len // tm
    if tiles_per_seq == 1:
        return act
    conv_w, conv_b = _gate_up_blocks(conv_w), _gate_up_blocks(conv_b)
    inner = tiles_per_seq - 1
    fix_rows = 2 * SUBLANE
    lo_tile = lambda bd: (bd // inner) * tiles_per_seq + bd % inner
    act_blk = lambda bd, side: ((lo_tile(bd) + 1) * (tm // fix_rows) - 1 + side, 0)
    return pl.pallas_call(
        _edge_fix_kernel,
        out_shape=jax.ShapeDtypeStruct((t, f), BF16),
        grid=((t // s_len) * inner, 2),
        in_specs=[
            pl.BlockSpec((fix_rows, f), act_blk),
            pl.BlockSpec((EDGE_ROWS, 2 * f), lambda bd, side: (lo_tile(bd), 0)),
            pl.BlockSpec((EDGE_ROWS, 2 * f), lambda bd, side: (lo_tile(bd) + 1, 0)),
            pl.BlockSpec((3, 2 * f), lambda bd, side: (0, 0)),
            pl.BlockSpec((1, 2 * f), lambda bd, side: (0, 0)),
        ],
        out_specs=pl.BlockSpec((fix_rows, f), act_blk),
        input_output_aliases={0: 0},
        compiler_params=_params(("arbitrary", "arbitrary")),
        name="conv_edge_fix",
    )(act, edges, edges, conv_w, conv_b)


def _rope(x, cos, sin_signed):
    return x * cos + pltpu.roll(x, LANE // 2, 1) * sin_signed


def rope_tables(s_len, d):
    inv = 1.0 / (ROPE_THETA ** (jnp.arange(0, d, 2, dtype=F32) / d))
    ang = jnp.arange(s_len, dtype=F32)[:, None] * inv[None, :]
    cos, sin = jnp.cos(ang), jnp.sin(ang)
    pad = jnp.zeros((s_len, LANE // 2 - d // 2), F32)
    return (jnp.concatenate([cos, pad, cos, pad], axis=-1), jnp.concatenate([-sin, pad, sin, pad], axis=-1))


NA_GROUP = 4
NA_WIN_ROWS = 12
NA_FIRST, NA_INTERIOR, NA_LAST = 0, 1, 2


def _na_kernel(q_ref, k_ref, v_ref, bias_ref, o_ref, *, rows, unroll):
    scale = HEAD_DIM**-0.5
    n_groups = rows // NA_GROUP
    gq = NA_GROUP * GRID_W
    wk = NA_WIN_ROWS * GRID_W

    def one_group(g):
        r0 = g * NA_GROUP
        ws = jnp.clip(r0 - NA_ROWS // 2, 0, rows - NA_WIN_ROWS)
        variant = jnp.where(g == 0, NA_FIRST, jnp.where(g == n_groups - 1, NA_LAST, NA_INTERIOR))
        q0 = pl.multiple_of(r0 * GRID_W, gq)
        k0 = pl.multiple_of(ws * GRID_W, GRID_W)
        s = lax.dot_general(q_ref[pl.ds(q0, gq), :], k_ref[pl.ds(k0, wk), :], _NT, preferred_element_type=F32)
        s = s * (scale * LOG2_E) + bias_ref[variant]
        m = jnp.max(s, axis=-1, keepdims=True)
        p = jnp.exp2(s - m)
        l = jnp.sum(p, axis=-1, keepdims=True)
        o = jnp.dot(p.astype(BF16), v_ref[pl.ds(k0, wk), :], preferred_element_type=F32) / l
        o_ref[pl.ds(q0, gq), :] = o.astype(o_ref.dtype)

    def body(it, carry):
        for u in range(unroll):
            one_group(it * unroll + u)
        return carry

    lax.fori_loop(0, n_groups // unroll, body, 0)


def na_bias_tables(rpb_all):
    qc = np.arange(GRID_W)[:, None]
    kc = np.arange(GRID_W)[None, :]
    q_ws = np.clip(qc - NA_COLS // 2, 0, GRID_W - NA_COLS)
    valid = (kc >= q_ws) & (kc < q_ws + NA_COLS)
    coff = np.clip(kc - qc + NA_COLS - 1, 0, 2 * NA_COLS - 2)
    onehot = (coff[None] == np.arange(2 * NA_COLS - 1)[:, None, None]).astype(np.float32)
    e = jnp.einsum("lhrc,cqk->lhrqk", rpb_all.astype(F32), onehot, precision=lax.Precision.HIGHEST)
    e = jnp.where(valid, e * LOG2_E, NEG_INF)
    pad = NA_WIN_ROWS
    e_pad = jnp.pad(e, ((0, 0), (0, 0), (pad, pad), (0, 0), (0, 0)))
    i = np.arange(NA_WIN_ROWS)
    slabs, valid_rows = [], []
    for r0_minus_ws, first_valid in ((0, lambda u: 0), (NA_ROWS // 2, lambda u: u), (NA_ROWS, lambda u: NA_ROWS // 2)):
        for u in range(NA_GROUP):
            lo = pad + (NA_ROWS - 1) - u - r0_minus_ws
            slabs.append(e_pad[:, :, lo : lo + NA_WIN_ROWS])
            valid_rows.append((i >= first_valid(u)) & (i < first_valid(u) + NA_ROWS))
    tab = jnp.stack(slabs, axis=2)
    valid_rows = np.stack(valid_rows)[:, :, None, None]
    tab = jnp.where(valid_rows, tab, NEG_INF)
    tab = tab.transpose(0, 1, 2, 4, 3, 5)
    return tab.reshape(tab.shape[:2] + (3, NA_GROUP * GRID_W, NA_WIN_ROWS * GRID_W))


def na_attention(proj3, bias_tab):
    b, s_len, _ = proj3.shape
    rows = s_len // GRID_W
    assert rows >= NA_WIN_ROWS and rows % NA_GROUP == 0, "grid too short for the grouped neighbourhood kernel"
    n_groups = rows // NA_GROUP
    blk = lambda off: pl.BlockSpec((None, s_len, LANE), lambda bi, h, off=off: (bi, 0, off + h))
    return pl.pallas_call(
        functools.partial(_na_kernel, rows=rows, unroll=4 if n_groups % 4 == 0 else 1),
        out_shape=jax.ShapeDtypeStruct((b, s_len, A_W), BF16),
        grid=(b, NA_HEADS),
        in_specs=[
            blk(QA_BLK),
            blk(KA_BLK),
            blk(VA_BLK),
            pl.BlockSpec((None, 3, NA_GROUP * GRID_W, NA_WIN_ROWS * GRID_W), lambda bi, h: (h, 0, 0, 0)),
        ],
        out_specs=pl.BlockSpec((None, s_len, LANE), lambda bi, h: (bi, 0, h)),
        compiler_params=_params(("parallel", "parallel")),
        name="na_attention",
    )(proj3, proj3, proj3, bias_tab)


def _band_kernel(*refs, s_len, dil, tq, hw, win, has_sink, want_lse, unroll):
    refs = list(refs)
    q_ref, k_ref, v_ref, cos_ref, sin_ref, mask_ref = refs[:6]
    del refs[:6]
    sink_ref = refs.pop(0) if has_sink else None
    o_ref = refs.pop(0)
    lse_ref = refs.pop(0) if want_lse else None
    stage, qd, kd, vd = refs[:4]
    od = refs[4] if dil > 1 else None
    ld = refs[5] if (dil > 1 and want_lse) else None
    sub = s_len // dil
    nq = sub // tq
    scale = HEAD_DIM**-0.5

    two_pass = dil > 4 and dil % 4 == 0
    stage2 = refs[-1] if two_pass else None
    quarter = s_len // 4
    d2 = dil // 4

    def split_classes(put):
        if not two_pass:
            for r in range(dil):
                put(r, stage[pl.ds(r, sub, stride=dil), :])
            return
        for r1 in range(4):
            stage2[pl.ds(r1 * quarter, quarter), :] = stage[pl.ds(r1, quarter, stride=4), :]
        for r1 in range(4):
            for r2 in range(d2):
                put(r1 + 4 * r2, stage2[pl.ds(r1 * quarter + r2, sub, stride=d2), :])

    def join_classes(get, dst_ref):
        if not two_pass:
            for r in range(dil):
                dst_ref[pl.ds(r, sub, stride=dil), :] = get(r)
            return
        for r1 in range(4):
            for r2 in range(d2):
                stage2[pl.ds(r1 * quarter + r2, sub, stride=d2), :] = get(r1 + 4 * r2)
        for r1 in range(4):
            dst_ref[pl.ds(r1, quarter, stride=4), :] = stage2[pl.ds(r1 * quarter, quarter), :]

    def regroup(val, dst):
        if dil == 1:
            dst[0] = val.astype(BF16)
        else:
            stage[...] = val

            def put(r, rows):
                dst[r] = rows.astype(BF16)

            split_classes(put)

    cos, sin = cos_ref[...], sin_ref[...]
    regroup(_rope(k_ref[...].astype(F32), cos, sin), kd)
    regroup(_rope(q_ref[...].astype(F32), cos, sin), qd)
    regroup(v_ref[...].astype(F32), vd)

    def one_block(it):
        r = it // nq
        q0 = pl.multiple_of((it % nq) * tq, tq)
        ws = pl.multiple_of(jnp.clip(q0 - hw, 0, sub - win), 64)
        s = lax.dot_general(qd[r, pl.ds(q0, tq), :], kd[r, pl.ds(ws, win), :], _NT, preferred_element_type=F32)
        s = s * (scale * LOG2_E) + mask_ref[(q0 - ws) // hw]
        m = jnp.max(s, axis=-1, keepdims=True)
        if has_sink:
            sk = sink_ref[0:1, 0:1] * LOG2_E
            m = jnp.maximum(m, sk)
        p = jnp.exp2(s - m)
        l = jnp.sum(p, axis=-1, keepdims=True)
        if has_sink:
            l = l + jnp.exp2(sk - m)
        o = jnp.dot(p.astype(BF16), vd[r, pl.ds(ws, win), :], preferred_element_type=F32) / l
        if dil == 1:
            o_ref[pl.ds(q0, tq), :] = o.astype(o_ref.dtype)
        else:
            od[r, pl.ds(q0, tq), :] = o
        if want_lse:
            lse = jnp.broadcast_to(m * LN_2 + jnp.log(l), (tq, LANE))
            if dil == 1:
                lse_ref[pl.ds(q0, tq), :] = lse
            else:
                ld[r, pl.ds(q0, tq), :] = lse

    def body(g, carry):
        for u in range(unroll):
            one_block(g * unroll + u)
        return carry

    lax.fori_loop(0, dil * nq // unroll, body, 0)

    if dil > 1:
        join_classes(lambda r: od[r], stage)
        o_ref[...] = stage[...].astype(o_ref.dtype)
        if want_lse:
            join_classes(lambda r: ld[r], lse_ref)


def banded_attention(proj3, cos, sin, *, dil, hw, n_heads, q_blk, k_blk, v_blk, kv_group, sink=None, want_lse=False):
    b, s_len, _ = proj3.shape
    sub = s_len // dil
    tq = min(max(128, 2 * hw), sub)
    win = min(tq + 2 * hw, sub)
    total = dil * (sub // tq)
    unroll = 4 if total % 4 == 0 else 1
    out_w = n_heads * LANE

    def col(off, group=1):
        return pl.BlockSpec((None, s_len, LANE), lambda bi, h: (bi, 0, off + h // group))

    offsets = [q0 - min(max(q0 - hw, 0), sub - win) for q0 in range(0, sub, tq)]
    rel = np.arange(win)[None, :] - np.arange(tq)[:, None]
    band = np.stack([np.where(np.abs(rel - off) <= hw, 0.0, NEG_INF) for off in range(0, max(offsets) + 1, hw)])
    band = jnp.asarray(band, F32)

    table = pl.BlockSpec((s_len, LANE), lambda bi, h: (0, 0))
    in_specs = [col(q_blk), col(k_blk, kv_group), col(v_blk, kv_group), table, table]
    in_specs.append(pl.BlockSpec(band.shape, lambda bi, h: (0, 0, 0)))
    args = [proj3, proj3, proj3, cos, sin, band]
    if sink is not None:
        in_specs.append(pl.BlockSpec((None, SUBLANE, LANE), lambda bi, h: (h, 0, 0)))
        args.append(jnp.broadcast_to(sink.astype(F32)[:, None, None], (n_heads, SUBLANE, LANE)))
    out_spec = pl.BlockSpec((None, s_len, LANE), lambda bi, h: (bi, 0, h))
    out_shape = [jax.ShapeDtypeStruct((b, s_len, out_w), BF16)]
    out_specs = [out_spec]
    if want_lse:
        out_shape.append(jax.ShapeDtypeStruct((b, s_len, out_w), F32))
        out_specs.append(out_spec)
    slab = (dil, sub, LANE)
    scratch = [pltpu.VMEM((s_len, LANE), F32), pltpu.VMEM(slab, BF16), pltpu.VMEM(slab, BF16), pltpu.VMEM(slab, BF16)]
    if dil > 1:
        scratch.append(pltpu.VMEM(slab, F32))
        if want_lse:
            scratch.append(pltpu.VMEM(slab, F32))
    if dil > 4 and dil % 4 == 0:
        scratch.append(pltpu.VMEM((s_len, LANE), F32))
    return pl.pallas_call(
        functools.partial(
            _band_kernel, s_len=s_len, dil=dil, tq=tq, hw=hw, win=win, has_sink=sink is not None,
            want_lse=want_lse, unroll=unroll,
        ),
        out_shape=out_shape,
        grid=(b, n_heads),
        in_specs=in_specs,
        out_specs=out_specs,
        scratch_shapes=scratch,
        compiler_params=_params(("parallel", "arbitrary")),
        name=f"banded_attention_d{dil}",
    )(*args)


def _merge_kernel(o0_ref, o1_ref, o2_ref, l0_ref, l1_ref, l2_ref, out_ref):
    l0, l1, l2 = l0_ref[...], l1_ref[...], l2_ref[...]
    mx = jnp.maximum(jnp.maximum(l0, l1), l2)
    e0, e1, e2 = jnp.exp(l0 - mx), jnp.exp(l1 - mx), jnp.exp(l2 - mx)
    den = e0 + e1 + e2
    gw = DIL_HPG * LANE
    for g, (o_ref, e) in enumerate(((o0_ref, e0), (o1_ref, e1), (o2_ref, e2))):
        out_ref[:, g * gw : (g + 1) * gw] = (o_ref[...].astype(F32) * (e / den)).astype(out_ref.dtype)


def merge_dilated(outs, lses):
    t, gw = outs[0].shape
    tm = min(512, t)
    spec = pl.BlockSpec((tm, gw), lambda i: (i, 0))
    return pl.pallas_call(
        _merge_kernel,
        out_shape=jax.ShapeDtypeStruct((t, len(outs) * gw), BF16),
        grid=(t // tm,),
        in_specs=[spec] * 6,
        out_specs=pl.BlockSpec((tm, len(outs) * gw), lambda i: (i, 0)),
        compiler_params=_params(("parallel",)),
        name="merge_dilated",
    )(*outs, *lses)


def _mla_pre_kernel(h_ref, ssq_ref, w1_ref, gq_ref, gkv_ref, wuq_ref, wukv_ref, cos_ref, sin_ref, q_ref, k_ref, v_ref):
    c = jnp.dot(h_ref[...], w1_ref[...], preferred_element_type=F32) * _row_scale(ssq_ref, h_ref.shape[1])
    cq = c[:, :MLA_Q_RANK]
    ckv = c[:, MLA_Q_RANK : MLA_Q_RANK + KV_PAD]
    kr = c[:, MLA_Q_RANK + KV_PAD :]
    cos, sin = cos_ref[...], sin_ref[...]

    nq = (cq * lax.rsqrt(jnp.mean(cq * cq, axis=-1, keepdims=True) + NORM_EPS)) * gq_ref[...]
    q = jnp.dot(nq.astype(BF16), wuq_ref[...], preferred_element_type=F32) * MLA_Q_SCALE
    ms = jnp.sum(ckv * ckv, axis=-1, keepdims=True) * (1.0 / MLA_KV_RANK)
    nkv = (ckv * lax.rsqrt(ms + NORM_EPS)) * gkv_ref[...]
    kv = jnp.dot(nkv.astype(BF16), wukv_ref[...], preferred_element_type=F32)
    k_rope = _rope(kr, cos, sin).astype(k_ref.dtype)
    for h in range(MLA_HEADS):
        lo = h * MLA_QK
        q_ref[:, lo : lo + LANE] = q[:, lo : lo + LANE].astype(q_ref.dtype)
        q_ref[:, lo + LANE : lo + MLA_QK] = _rope(q[:, lo + LANE : lo + MLA_QK], cos, sin).astype(q_ref.dtype)
        k_ref[:, lo : lo + LANE] = kv[:, h * LANE : (h + 1) * LANE].astype(k_ref.dtype)
        k_ref[:, lo + LANE : lo + MLA_QK] = k_rope
        v_ref[:, lo : lo + LANE] = kv[:, D_OUT_W + h * LANE : D_OUT_W + (h + 1) * LANE].astype(v_ref.dtype)
        v_ref[:, lo + LANE : lo + MLA_QK] = jnp.ones((v_ref.shape[0], LANE), v_ref.dtype)


def mla_pre(h, ssq, w1, gq, gkv, wuq, wukv, cos, sin, s_len):
    t, d = h.shape
    tm = min(256, s_len)
    n_pos = s_len // tm
    const = lambda shape: pl.BlockSpec(shape, lambda i: (0, 0))
    row = lambda w: pl.BlockSpec((tm, w), lambda i: (i, 0))
    return pl.pallas_call(
        _mla_pre_kernel,
        out_shape=[
            jax.ShapeDtypeStruct((t, MLA_HEADS * MLA_QK), BF16),
            jax.ShapeDtypeStruct((t, MLA_HEADS * MLA_QK), BF16),
            jax.ShapeDtypeStruct((t, MLA_HEADS * MLA_QK), BF16),
        ],
        grid=(t // tm,),
        in_specs=[
            row(d),
            row(LANE),
            const((d, MLA_IN_W)),
            const((1, MLA_Q_RANK)),
            const((1, KV_PAD)),
            const((MLA_Q_RANK, MLA_HEADS * MLA_QK)),
            const((KV_PAD, 2 * D_OUT_W)),
            pl.BlockSpec((tm, LANE), lambda i: (i % n_pos, 0)),
            pl.BlockSpec((tm, LANE), lambda i: (i % n_pos, 0)),
        ],
        out_specs=[row(MLA_HEADS * MLA_QK), row(MLA_HEADS * MLA_QK), row(MLA_HEADS * MLA_QK)],
        compiler_params=_params(("parallel",)),
        name="mla_pre",
    )(h, ssq, w1, gq, gkv, wuq, wukv, cos, sin)


def _mla_kernel(q_ref, k_ref, v_ref, o_ref, s_a, s_b, p_a, p_b, *, tq):
    n_blk = q_ref.shape[0] // tq

    def scores(i, s_dst):
        q0 = pl.multiple_of(i * tq, tq)
        s_dst[...] = lax.dot_general(q_ref[pl.ds(q0, tq), :], k_ref[...], _NT, preferred_element_type=F32)

    def softmax(s_src, p_dst):
        s = s_src[...]
        p_dst[...] = jnp.exp2(s - jnp.max(s, axis=-1, keepdims=True)).astype(p_dst.dtype)

    def values(i, p_src):
        ov = jnp.dot(p_src[...], v_ref[...], preferred_element_type=F32)
        o = ov[:, :LANE] / ov[:, LANE : LANE + 1]
        o_ref[pl.ds(pl.multiple_of(i * tq, tq), tq), :] = o.astype(o_ref.dtype)

    scores(0, s_a)
    scores(1, s_b)
    softmax(s_a, p_a)

    for t in range(0, n_blk - 2, 2):
        values(t, p_a)
        scores(t + 2, s_a)
        softmax(s_b, p_b)
        values(t + 1, p_b)
        scores(t + 3, s_b)
        softmax(s_a, p_a)
    values(n_blk - 2, p_a)
    softmax(s_b, p_b)
    values(n_blk - 1, p_b)


def mla_attention(q, k, v):
    b, s_len, _ = q.shape
    tq = min(512, s_len // 2)
    assert (s_len // tq) % 2 == 0, "the pipeline alternates two buffer sets over an even block count"
    blk = lambda w: pl.BlockSpec((None, s_len, w), lambda bi, h: (bi, 0, h))
    return pl.pallas_call(
        functools.partial(_mla_kernel, tq=tq),
        out_shape=jax.ShapeDtypeStruct((b, s_len, D_OUT_W), BF16),
        grid=(b, MLA_HEADS),
        in_specs=[blk(MLA_QK), blk(MLA_QK), blk(MLA_QK)],
        out_specs=blk(LANE),
        scratch_shapes=[
            pltpu.VMEM((tq, s_len), F32),
            pltpu.VMEM((tq, s_len), F32),
            pltpu.VMEM((tq, s_len), BF16),
            pltpu.VMEM((tq, s_len), BF16),
        ],
        compiler_params=_params(("parallel", "parallel")),
        name="mla_attention",
    )(q, k, v)


def _pad_cols(w, width):
    return jnp.pad(w, ((0, 0),) * (w.ndim - 1) + ((0, width - w.shape[-1]),))


def _interleave_rope(w):
    z = jnp.zeros(w.shape[:-1] + (LANE // 2 - MLA_ROPE // 2,), w.dtype)
    return jnp.concatenate([w[..., : MLA_ROPE // 2], z, w[..., MLA_ROPE // 2 :], z], axis=-1)


def _gate_up_blocks(w):
    f = w.shape[-1] // 2
    lead = w.shape[:-1]
    gate = w[..., :f].reshape(lead + (f // FF_HALF, FF_HALF))
    up = w[..., f:].reshape(lead + (f // FF_HALF, FF_HALF))
    return jnp.stack([gate, up], axis=-2).reshape(lead + (2 * f,))


def prepare_layer(w_in_mla, w_uq, mla_kv_norm, w_ukv, conv_w, conv_b):
    w_mla = jnp.concatenate(
        [
            w_in_mla[:, :MLA_Q_RANK],
            _pad_cols(w_in_mla[:, MLA_Q_RANK : MLA_Q_RANK + MLA_KV_RANK], KV_PAD),
            _interleave_rope(w_in_mla[:, MLA_Q_RANK + MLA_KV_RANK :]),
        ],
        axis=-1,
    )
    uq = w_uq.reshape(MLA_Q_RANK, MLA_HEADS, MLA_NOPE + MLA_ROPE)
    wuq = jnp.concatenate([uq[..., :MLA_NOPE], _interleave_rope(uq[..., MLA_NOPE:])], axis=-1)
    wuq = wuq.reshape(MLA_Q_RANK, MLA_HEADS * MLA_QK).astype(BF16)
    ukv = w_ukv.reshape(MLA_KV_RANK, MLA_HEADS, MLA_NOPE + MLA_V)
    wukv = jnp.concatenate(
        [ukv[..., :MLA_NOPE].reshape(MLA_KV_RANK, D_OUT_W), ukv[..., MLA_NOPE:].reshape(MLA_KV_RANK, D_OUT_W)], axis=-1
    )
    wukv = jnp.pad(wukv, ((0, KV_PAD - MLA_KV_RANK), (0, 0))).astype(BF16)
    gkv = _pad_cols(mla_kv_norm.reshape(1, MLA_KV_RANK), KV_PAD)
    return dict(w_mla=w_mla, wuq=wuq, wukv=wukv, gkv=gkv, conv_w=conv_w, conv_b=conv_b[None])


def encoder_layer(x, h, ssq, b, s_len, big, l, lw, na_bias, sink, mla_q_norm, ffn_norm, next_attn_norm, tabs):
    t, d = x.shape
    cos128, sin128, cos64, sin64 = tabs
    proj = matmul_scaled(h, big["w_in"], l, ssq, BF16, tm=MXU_TILE, tn=MXU_TILE, n_out=MAIN_W)
    proj3 = proj.reshape(b, s_len, proj.shape[1])

    oa = na_attention(proj3, na_bias)

    (ob,) = banded_attention(
        proj3, cos128, sin128, dil=1, hw=SW_WINDOW, n_heads=SW_HEADS, q_blk=QB_BLK, k_blk=KB_BLK, v_blk=VB_BLK,
        kv_group=SW_GROUP, sink=sink,
    )

    outs, lses = [], []
    for gi, (win, dil) in enumerate(DIL_PAIRS):
        o_g, lse_g = banded_attention(
            proj3, cos128, sin128, dil=dil, hw=(win // 2) // dil, n_heads=DIL_HPG, q_blk=QC_BLK + gi * DIL_HPG,
            k_blk=KC_BLK + gi * DIL_HPG, v_blk=VC_BLK + gi * DIL_HPG, kv_group=1, want_lse=True,
        )
        outs.append(o_g.reshape(t, DIL_HPG * LANE))
        lses.append(lse_g.reshape(t, DIL_HPG * LANE))
    oc = merge_dilated(outs, lses)

    q, k, v = mla_pre(
        h, ssq, lw["w_mla"], mla_q_norm.reshape(1, MLA_Q_RANK), lw["gkv"], lw["wuq"], lw["wukv"], cos64, sin64, s_len
    )
    od = mla_attention(q.reshape(b, s_len, -1), k.reshape(b, s_len, -1), v.reshape(b, s_len, -1))

    mix = jnp.concatenate([oa.reshape(t, A_W), ob.reshape(t, B_QW), oc, od.reshape(t, D_OUT_W)], axis=-1)
    x, h2, ssq2 = matmul_residual(mix, big["w_out"], l, x, g_next=ffn_norm)

    act = up_conv_gate(h2, ssq2, big["w_up"], l, lw["conv_w"], lw["conv_b"], s_len)
    half = MXU_TILE // 2
    if next_attn_norm is None:
        return matmul_residual(act, big["w_down"], l, x, tm=half, tn=half), None, None
    return matmul_residual(act, big["w_down"], l, x, g_next=next_attn_norm, tm=half, tn=half)


def trunk(x3, big, layers, na_bias, attn_norm, sink, mla_q_norm, ffn_norm, final_norm):
    b, s_len, d = x3.shape
    tabs = rope_tables(s_len, HEAD_DIM) + rope_tables(s_len, MLA_ROPE)
    x = x3.reshape(b * s_len, d)
    h, ssq = prenorm_rows(x, attn_norm[0])
    for l, lw in enumerate(layers):
        next_attn_norm = attn_norm[l + 1] if l + 1 < len(layers) else None
        x, h, ssq = encoder_layer(
            x, h, ssq, b, s_len, big, l, lw, na_bias[l], sink[l], mla_q_norm[l], ffn_norm[l], next_attn_norm, tabs
        )
    return rms_rows(x, final_norm, F32).reshape(b, s_len, d)


def kernel(x_prompt, x_sample, attn_norm, w_in, na_rpb, sink, mla_q_norm, w_uq, mla_kv_norm, w_ukv, w_out, ffn_norm, w_up, conv_w, conv_b, w_down, final_norm):
    depth = w_in.shape[0]
    assert w_down.shape[1] % FF_HALF == 0, "d_ff must be a whole number of gate column blocks"
    big = dict(w_in=w_in.astype(BF16), w_out=w_out.astype(BF16), w_up=w_up.astype(BF16), w_down=w_down.astype(BF16))
    layers = [
        prepare_layer(big["w_in"][l, :, MAIN_W:], w_uq[l], mla_kv_norm[l], w_ukv[l], conv_w[l], conv_b[l])
        for l in range(depth)
    ]
    na_bias = na_bias_tables(na_rpb)
    return tuple(
        trunk(x3, big, layers, na_bias, attn_norm, sink, mla_q_norm, ffn_norm, final_norm)
        for x3 in (x_prompt, x_sample)
    )
```
